```python
import math
import jax
import jax.numpy as jnp
from jax import lax
import numpy as np

D_MODEL = 2048
BATCH = 1
SEQ = 8192
DEPTH = 1
DEC_BATCH = 32
DEC_SEQ = 8
PAST_LEN = 16384
PAGE_SIZE = 128

ATT_HEADS = 16
ATT_HEAD_DIM = 64
ATT_WIDTH = ATT_HEADS * ATT_HEAD_DIM
Q_BLOCK = 128
SSD_HEADS = 16
SSD_HEAD_DIM = 64
SSD_WIDTH = SSD_HEADS * SSD_HEAD_DIM
SSD_GROUPS = 2
SSD_STATE = 128
SSD_CONV = 4
SSD_CHUNK = 128
CONV_DIM = SSD_WIDTH + 2 * SSD_GROUPS * SSD_STATE
MIX_WIDTH = ATT_WIDTH + SSD_WIDTH
IN_PROJ = 3 * ATT_WIDTH + SSD_WIDTH + CONV_DIM + SSD_HEADS
D_FF = 5632
FFN_CONV = 3
PLE_DIM = 256
RMS_EPS = 1e-6

kernel_name = 'stickbreak_ssd_hybrid_step'


def rmsnorm(x, g):
    xf = x.astype(jnp.float32)
    y = xf * lax.rsqrt(jnp.mean(xf * xf, axis=-1, keepdims=True) + RMS_EPS)
    return (y * g.astype(jnp.float32)).astype(x.dtype)


def causal_dwconv(x, buf, w, b):
    width = w.shape[0]
    n = x.shape[1]
    xp = jnp.concatenate([buf.astype(x.dtype), x], axis=1)
    y = b + xp[:, 0:n] * w[0]
    for j in range(1, width):
        y = y + xp[:, j:j + n] * w[j]
    return y, xp[:, xp.shape[1] - (width - 1):]


def stick_breaking_block(q, k, v, q_pos, k_pos, bias):
    z = (jnp.einsum('bqhd,bkhd->bhqk', q.astype(jnp.float32), k.astype(jnp.float32)) * (ATT_HEAD_DIM ** -0.5)
         + bias.astype(jnp.float32)[None, :, None, None])
    mask = k_pos[None, :] < q_pos[:, None]
    log_keep = jnp.where(mask, jax.nn.log_sigmoid(-z), 0.0)
    log_after = lax.cumsum(log_keep, axis=3, reverse=True) - log_keep
    attn = jnp.where(mask, jnp.exp(jax.nn.log_sigmoid(z) + log_after), 0.0)
    return jnp.einsum('bhqk,bkhd->bqhd', attn, v.astype(jnp.float32))


def stick_breaking_attention(q, k, v, bias):
    b, lq, h, d = q.shape
    lk = k.shape[1]
    k_pos = jnp.arange(lk, dtype=jnp.int32)
    q_pos = jnp.arange(lk - lq, lk, dtype=jnp.int32)
    if lq > Q_BLOCK and lq % Q_BLOCK == 0:
        nb = lq // Q_BLOCK
        qb = jnp.swapaxes(q.reshape(b, nb, Q_BLOCK, h, d), 0, 1)
        pb = q_pos.reshape(nb, Q_BLOCK)
        out = lax.map(lambda blk: stick_breaking_block(blk[0], k, v, blk[1], k_pos, bias), (qb, pb))
        out = jnp.swapaxes(out, 0, 1).reshape(b, lq, h, d)
    else:
        out = stick_breaking_block(q, k, v, q_pos, k_pos, bias)
    return out.astype(q.dtype)


def chunk_decay_matrix(a_cum):
    n = a_cum.shape[-1]
    tri = jnp.tril(jnp.ones((n, n), dtype=bool))
    diff = a_cum[..., :, None] - a_cum[..., None, :]
    return jnp.where(tri, jnp.exp(jnp.where(tri, diff, 0.0)), 0.0)


def ssd_scan(x, dt, a, bmat, cmat, init_state):
    b, l, h, p = x.shape
    rep = h // bmat.shape[2]
    bh = jnp.repeat(bmat, rep, axis=2)
    ch = jnp.repeat(cmat, rep, axis=2)
    cs = SSD_CHUNK if l % SSD_CHUNK == 0 else math.gcd(l, SSD_CHUNK)
    nc = l // cs

    def chunks(t):
        return t.reshape((b, nc, cs) + t.shape[2:])

    xd = chunks(x * dt[..., None])
    bc, cc = chunks(bh), chunks(ch)
    a_cum = jnp.moveaxis(jnp.cumsum(chunks(dt * a), axis=2), 3, 1)
    scores = jnp.einsum('bcihn,bcjhn->bhcij', cc, bc) * chunk_decay_matrix(a_cum)
    y_diag = jnp.einsum('bhcij,bcjhp->bcihp', scores, xd)
    to_end = jnp.exp(a_cum[..., -1:] - a_cum)
    chunk_states = jnp.einsum('bcjhn,bhcj,bcjhp->cbhpn', bc, to_end, xd)
    chunk_decay = jnp.moveaxis(jnp.exp(a_cum[..., -1]), 2, 0)

    def step(state, inp):
        st, dec = inp
        return state * dec[..., None, None] + st, state

    final, start_states = lax.scan(step, init_state, (chunk_states, chunk_decay))
    y_off = jnp.einsum('bcihn,cbhpn,bhci->bcihp', cc, start_states, jnp.exp(a_cum))
    return (y_diag + y_off).reshape(b, l, h, p), final


def hybrid_layer(h, p, k_past, v_past, ssm0, conv0, ffn0, lw):
    b, l, _ = h.shape
    u = rmsnorm(h, lw['g_pre_mix'])
    proj = u @ lw['w_in']
    q, k, v, z, xbc, dt_raw = jnp.split(
        proj,
        [ATT_WIDTH, 2 * ATT_WIDTH, 3 * ATT_WIDTH, 3 * ATT_WIDTH + SSD_WIDTH, 3 * ATT_WIDTH + SSD_WIDTH + CONV_DIM],
        axis=-1)
    q = q.reshape(b, l, ATT_HEADS, ATT_HEAD_DIM)
    k = k.reshape(b, l, ATT_HEADS, ATT_HEAD_DIM)
    v = v.reshape(b, l, ATT_HEADS, ATT_HEAD_DIM)
    k_all = k if k_past is None else jnp.concatenate([k_past.astype(k.dtype), k], axis=1)
    v_all = v if v_past is None else jnp.concatenate([v_past.astype(v.dtype), v], axis=1)
    o_att = stick_breaking_attention(q, k_all, v_all, lw['sb_bias']).reshape(b, l, ATT_WIDTH)
    xbc, conv_new = causal_dwconv(xbc, conv0, lw['w_conv'], lw['b_conv'])
    xbc = jax.nn.silu(xbc.astype(jnp.float32))
    xs, bm, cm = jnp.split(xbc, [SSD_WIDTH, SSD_WIDTH + SSD_GROUPS * SSD_STATE], axis=-1)
    xs = xs.reshape(b, l, SSD_HEADS, SSD_HEAD_DIM)
    dt = jax.nn.softplus(dt_raw.astype(jnp.float32) + lw['dt_bias'].astype(jnp.float32))
    a = -jnp.exp(lw['a_log'].astype(jnp.float32))
    y, ssm_new = ssd_scan(xs, dt, a,
                          bm.reshape(b, l, SSD_GROUPS, SSD_STATE),
                          cm.reshape(b, l, SSD_GROUPS, SSD_STATE),
                          ssm0.astype(jnp.float32))
    y = (y + lw['d_skip'].astype(jnp.float32)[:, None] * xs).reshape(b, l, SSD_WIDTH)
    y = rmsnorm(y * jax.nn.silu(z.astype(jnp.float32)), lw['g_ssd_norm']).astype(h.dtype)
    mix = jnp.concatenate([o_att, y], axis=-1) @ lw['w_out']
    h = h + rmsnorm(mix, lw['g_post_mix'])
    up = rmsnorm(h, lw['g_pre_ffn']) @ lw['w_up']
    up, ffn_new = causal_dwconv(up, ffn0, lw['w_ffn_conv'], lw['b_ffn_conv'])
    gate, val = jnp.split(up, 2, axis=-1)
    f = (jax.nn.gelu(gate, approximate=True) * val) @ lw['w_down']
    h = h + rmsnorm(f, lw['g_post_ffn'])
    ple_gate = jax.nn.sigmoid((h @ lw['w_ple_gate']).astype(jnp.float32))
    h = h + ((p @ lw['w_ple_proj']).astype(jnp.float32) * ple_gate).astype(h.dtype)
    return h, (k, v, ssm_new, conv_new, ffn_new)


def setup_inputs(seed: int = 0) -> dict:
    key = jax.random.key(seed)
    ks = jax.random.split(key, 32)
    n_pages = PAST_LEN // PAGE_SIZE
    n_used = DEC_BATCH * n_pages
    n_pool = n_used + max(1, n_used // 4)

    def nrm(k, shape, scale=1.0):
        return jax.random.normal(k, shape, jnp.float32) * scale

    def gain(k, n):
        return 1.0 + 0.05 * nrm(k, (DEPTH, n))

    kv_shape = (DEPTH, n_pool, PAGE_SIZE, ATT_HEADS, ATT_HEAD_DIM)
    page_table = jax.random.permutation(ks[7], n_pool)[:n_used].reshape(DEC_BATCH, n_pages).astype(jnp.int32)
    dt0 = jnp.exp(jax.random.uniform(ks[13], (DEPTH, SSD_HEADS), jnp.float32, math.log(1e-3), math.log(1e-1)))
    return {
        'x_prompt': nrm(ks[0], (BATCH, SEQ, D_MODEL)),
        'x_sample': nrm(ks[1], (DEC_BATCH, DEC_SEQ, D_MODEL)),
        'cache_k': nrm(ks[2], kv_shape),
        'cache_v': nrm(ks[3], kv_shape),
        'state_ssm': nrm(ks[4], (DEPTH, DEC_BATCH, SSD_HEADS, SSD_HEAD_DIM, SSD_STATE), 0.1),
        'state_conv': nrm(ks[5], (DEPTH, DEC_BATCH, SSD_CONV - 1, CONV_DIM)),
        'state_ffn_conv': nrm(ks[6], (DEPTH, DEC_BATCH, FFN_CONV - 1, 2 * D_FF)),
        'page_table': page_table,
        'p_prompt': nrm(ks[8], (DEPTH, BATCH, SEQ, PLE_DIM)),
        'p_sample': nrm(ks[9], (DEPTH, DEC_BATCH, DEC_SEQ, PLE_DIM)),
        'g_pre_mix': gain(ks[10], D_MODEL),
        'w_in': nrm(ks[11], (DEPTH, D_MODEL, IN_PROJ), D_MODEL ** -0.5),
        'sb_bias': jax.random.uniform(ks[28], (DEPTH, ATT_HEADS), jnp.float32, -8.0, -6.0),
        'w_conv': nrm(ks[12], (DEPTH, SSD_CONV, CONV_DIM), SSD_CONV ** -0.5),
        'b_conv': nrm(ks[14], (DEPTH, CONV_DIM), 0.01),
        'dt_bias': dt0 + jnp.log(-jnp.expm1(-dt0)),
        'a_log': jnp.log(jax.random.uniform(ks[15], (DEPTH, SSD_HEADS), jnp.float32, 1.0, 16.0)),
        'd_skip': 1.0 + 0.1 * nrm(ks[16], (DEPTH, SSD_HEADS)),
        'g_ssd_norm': gain(ks[17], SSD_WIDTH),
        'w_out': nrm(ks[18], (DEPTH, MIX_WIDTH, D_MODEL), MIX_WIDTH ** -0.5),
        'g_post_mix': gain(ks[19], D_MODEL),
        'g_pre_ffn': gain(ks[20], D_MODEL),
        'w_up': nrm(ks[21], (DEPTH, D_MODEL, 2 * D_FF), D_MODEL ** -0.5),
        'w_ffn_conv': nrm(ks[22], (DEPTH, FFN_CONV, 2 * D_FF), FFN_CONV ** -0.5),
        'b_ffn_conv': nrm(ks[23], (DEPTH, 2 * D_FF), 0.01),
        'w_down': nrm(ks[24], (DEPTH, D_FF, D_MODEL), D_FF ** -0.5),
        'g_post_ffn': gain(ks[25], D_MODEL),
        'w_ple_gate': nrm(ks[26], (DEPTH, D_MODEL, D_MODEL), D_MODEL ** -0.5),
        'w_ple_proj': nrm(ks[27], (DEPTH, PLE_DIM, D_MODEL), PLE_DIM ** -0.5),
    }


def reference(x_prompt, x_sample, cache_k, cache_v, state_ssm, state_conv, state_ffn_conv, page_table,
              p_prompt, p_sample, g_pre_mix, w_in, sb_bias, w_conv, b_conv, dt_bias, a_log, d_skip, g_ssd_norm,
              w_out, g_post_mix, g_pre_ffn, w_up, w_ffn_conv, b_ffn_conv, w_down, g_post_ffn,
              w_ple_gate, w_ple_proj):
    bp = x_prompt.shape[0]
    db = x_sample.shape[0]
    past_len = page_table.shape[1] * PAGE_SIZE
    hp, hs = x_prompt, x_sample
    outs_p, outs_s = [], []
    for i in range(DEPTH):
        lw = {
            'g_pre_mix': g_pre_mix[i], 'w_in': w_in[i], 'sb_bias': sb_bias[i], 'w_conv': w_conv[i],
            'b_conv': b_conv[i], 'dt_bias': dt_bias[i], 'a_log': a_log[i], 'd_skip': d_skip[i],
            'g_ssd_norm': g_ssd_norm[i], 'w_out': w_out[i], 'g_post_mix': g_post_mix[i],
            'g_pre_ffn': g_pre_ffn[i], 'w_up': w_up[i], 'w_ffn_conv': w_ffn_conv[i],
            'b_ffn_conv': b_ffn_conv[i], 'w_down': w_down[i], 'g_post_ffn': g_post_ffn[i],
            'w_ple_gate': w_ple_gate[i], 'w_ple_proj': w_ple_proj[i],
        }
        ssm0 = jnp.zeros((bp, SSD_HEADS, SSD_HEAD_DIM, SSD_STATE), jnp.float32)
        conv0 = jnp.zeros((bp, SSD_CONV - 1, CONV_DIM), x_prompt.dtype)
        ffn0 = jnp.zeros((bp, FFN_CONV - 1, 2 * D_FF), x_prompt.dtype)
        hp, st_p = hybrid_layer(hp, p_prompt[i], None, None, ssm0, conv0, ffn0, lw)
        k_past = cache_k[i][page_table].reshape(db, past_len, ATT_HEADS, ATT_HEAD_DIM)
        v_past = cache_v[i][page_table].reshape(db, past_len, ATT_HEADS, ATT_HEAD_DIM)
        hs, st_s = hybrid_layer(hs, p_sample[i], k_past, v_past, state_ssm[i], state_conv[i],
                                state_ffn_conv[i], lw)
        outs_p.append(st_p)
        outs_s.append(st_s)

    def stack(outs, j):
        return jnp.stack([o[j] for o in outs], axis=0)

    return (hp, hs, stack(outs_p, 0), stack(outs_s, 0), stack(outs_p, 1), stack(outs_s, 1),
            stack(outs_p, 2), stack(outs_s, 2), stack(outs_p, 3), stack(outs_s, 3),
            stack(outs_p, 4), stack(outs_s, 4))
```

```python
import functools

import jax
import jax.numpy as jnp
from jax import lax
from jax.experimental import pallas as pl
from jax.experimental.pallas import tpu as pltpu

F32 = jnp.float32
BF16 = jnp.bfloat16

ATT_HEADS = 16
ATT_HEAD_DIM = 64
ATT_WIDTH = ATT_HEADS * ATT_HEAD_DIM
SSD_HEADS = 16
SSD_HEAD_DIM = 64
SSD_WIDTH = SSD_HEADS * SSD_HEAD_DIM
SSD_GROUPS = 2
SSD_STATE = 128
SSD_CONV = 4
SSD_CHUNK = 128
CONV_DIM = SSD_WIDTH + 2 * SSD_GROUPS * SSD_STATE
FFN_CONV = 3
RMS_EPS = 1e-6
PAGE_SIZE = 128

LANES = 128
HEAD_PAIRS = ATT_WIDTH // LANES
VMEM_LIMIT = 56 * 1024 * 1024

ROW_TILE = 512
IN_PROJ_TILE = 256
ATT_TQ = 128
ATT_TK = 128
FFN_TF = 512
FFN_HALO = 16
PAGES_PER_STEP = 4


def _params(*sem):
    return pltpu.CompilerParams(dimension_semantics=sem, vmem_limit_bytes=VMEM_LIMIT)


def _rms(x):
    return x * lax.rsqrt(jnp.mean(x * x, axis=-1, keepdims=True) + RMS_EPS)


def _softplus(x):
    return jnp.maximum(x, 0.0) + jnp.log(1.0 + jnp.exp(-jnp.abs(x)))


def _silu(x):
    return x * (1.0 / (1.0 + jnp.exp(-x)))


def _dot(a, b):
    return jnp.dot(a, b, preferred_element_type=F32)


def _dot_nt(a, b):
    return lax.dot_general(a, b, (((1,), (1,)), ((), ())), preferred_element_type=F32)


def _dot_tn(a, b):
    return lax.dot_general(a, b, (((0,), (0,)), ((), ())), preferred_element_type=F32)


def _split3(x):
    hi = x.astype(BF16)
    r1 = x - hi.astype(F32)
    mid = r1.astype(BF16)
    lo = (r1 - mid.astype(F32)).astype(BF16)
    return hi, mid, lo


def _dot_exact_rhs(m, x):
    hi, mid, lo = _split3(x)
    return _dot(m, hi) + _dot(m, mid) + _dot(m, lo)


def _dot_exact_lhs(x, m):
    hi, mid, lo = _split3(x)
    return _dot(hi, m) + _dot(mid, m) + _dot(lo, m)


def _const_spec(shape):
    return pl.BlockSpec(shape, lambda *_: (0,) * len(shape))


def _in_proj_kernel(x_ref, g_ref, wqkv_ref, wz_ref, wxbc_ref, wdt_ref,
                    q_ref, kf_ref, vf_ref, kb_ref, vb_ref, z_ref, xbc_ref, dt_ref):
    u = (_rms(x_ref[...]) * g_ref[...]).astype(BF16)
    qkv = _dot(u, wqkv_ref[...])
    q_ref[...] = (qkv[:, :ATT_WIDTH] * (ATT_HEAD_DIM ** -0.5)).astype(q_ref.dtype)
    k = qkv[:, ATT_WIDTH:2 * ATT_WIDTH]
    v = qkv[:, 2 * ATT_WIDTH:]
    kf_ref[...] = k
    vf_ref[...] = v
    kb_ref[...] = k.astype(BF16)
    vb_ref[...] = v.astype(BF16)
    z_ref[...] = _dot(u, wz_ref[...])
    xbc_ref[...] = _dot(u, wxbc_ref[...])
    dt_ref[...] = _dot(u, wdt_ref[...])


def _in_proj(x, g, wqkv, wz, wxbc, wdt, q_dtype):
    rows, d = x.shape
    tm = min(IN_PROJ_TILE, rows)
    row = lambda n: pl.BlockSpec((tm, n), lambda i: (i, 0))
    res = lambda a: pl.BlockSpec(a.shape, lambda i: (0, 0), pipeline_mode=pl.Buffered(1))
    out_shape = (
        jax.ShapeDtypeStruct((rows, ATT_WIDTH), q_dtype),
        jax.ShapeDtypeStruct((rows, ATT_WIDTH), F32),
        jax.ShapeDtypeStruct((rows, ATT_WIDTH), F32),
        jax.ShapeDtypeStruct((rows, ATT_WIDTH), BF16),
        jax.ShapeDtypeStruct((rows, ATT_WIDTH), BF16),
        jax.ShapeDtypeStruct((rows, SSD_WIDTH), F32),
        jax.ShapeDtypeStruct((rows, CONV_DIM), F32),
        jax.ShapeDtypeStruct((rows, LANES), F32),
    )
    return pl.pallas_call(
        _in_proj_kernel,
        grid=(rows // tm,),
        in_specs=[row(d), res(g), res(wqkv), res(wz), res(wxbc), res(wdt)],
        out_specs=tuple(row(s.shape[1]) for s in out_shape),
        out_shape=out_shape,
        compiler_params=_params("parallel"),
        name="in_proj",
    )(x, g, wqkv, wz, wxbc, wdt)


def _stick_block(z, carry, tri, mask):
    tk = z.shape[1]
    sp = _softplus(z)
    if mask is not None:
        sp = jnp.where(mask, sp, 0.0)
    hi = sp.astype(BF16)
    lo = (sp - hi.astype(F32)).astype(BF16)
    cs = _dot(hi, tri) + _dot(lo, tri)
    p = jnp.exp(z - cs[:, :tk] - carry)
    if mask is not None:
        p = jnp.where(mask, p, 0.0)
    return p, carry + cs[:, tk:]


def _tri_ones(tk):
    s = lax.broadcasted_iota(jnp.int32, (tk, tk + LANES), 0)
    j = lax.broadcasted_iota(jnp.int32, (tk, tk + LANES), 1)
    return ((s >= j) | (j >= tk)).astype(BF16)


def _attn_prompt_kernel(bias_ref, q_ref, k_ref, v_ref, tri_ref, o_ref, acc_ref, c_ref):
    hp = pl.program_id(0)
    qi = pl.program_id(1)
    tq, tk = ATT_TQ, ATT_TK
    per = tq // tk
    low = lax.broadcasted_iota(jnp.int32, (1, LANES), 1) < ATT_HEAD_DIM
    q = q_ref[...]
    qz = jnp.zeros_like(q)
    qs = (jnp.where(low, q, qz), jnp.where(low, qz, q))
    bias = (bias_ref[2 * hp], bias_ref[2 * hp + 1])
    tri = tri_ref[...]
    acc_ref[...] = jnp.zeros_like(acc_ref)
    c_ref[...] = jnp.zeros_like(c_ref)
    q_pos = qi * tq + lax.broadcasted_iota(jnp.int32, (tq, tk), 0)
    k_off = lax.broadcasted_iota(jnp.int32, (tq, tk), 1)

    def block(j, masked):
        start = pl.multiple_of(j * tk, tk)
        kb = k_ref[pl.ds(start, tk), :]
        vb = v_ref[pl.ds(start, tk), :]
        vz = jnp.zeros_like(vb)
        vs = (jnp.where(low, vb, vz), jnp.where(low, vz, vb))
        mask = (j * tk + k_off) < q_pos if masked else None
        pv = None
        for a in range(2):
            z = _dot_nt(qs[a], kb) + bias[a]
            p, c_new = _stick_block(z, c_ref[a], tri, mask)
            c_ref[a] = c_new
            d = _dot(p.astype(BF16), vs[a])
            pv = d if pv is None else pv + d
        acc_ref[...] += pv

    for r in reversed(range(per)):
        block(qi * per + r, True)

    def body(jj, carry):
        block(qi * per - 1 - jj, False)
        return carry

    lax.fori_loop(0, qi * per, body, 0)
    o_ref[...] = acc_ref[...].astype(o_ref.dtype)


def _attn_prompt(bias, q, k, v):
    l = q.shape[0]
    tq = ATT_TQ
    grid_spec = pltpu.PrefetchScalarGridSpec(
        num_scalar_prefetch=1,
        grid=(HEAD_PAIRS, l // tq),
        in_specs=[
            pl.BlockSpec((tq, LANES), lambda hp, qi, b: (qi, hp)),
            pl.BlockSpec((l, LANES), lambda hp, qi, b: (0, hp)),
            pl.BlockSpec((l, LANES), lambda hp, qi, b: (0, hp)),
            pl.BlockSpec((ATT_TK, ATT_TK + LANES), lambda hp, qi, b: (0, 0)),
        ],
        out_specs=pl.BlockSpec((tq, LANES), lambda hp, qi, b: (qi, hp)),
        scratch_shapes=[pltpu.VMEM((tq, LANES), F32), pltpu.VMEM((2, tq, LANES), F32)],
    )
    return pl.pallas_call(
        _attn_prompt_kernel,
        grid_spec=grid_spec,
        out_shape=jax.ShapeDtypeStruct((l, ATT_WIDTH), BF16),
        compiler_params=_params("parallel", "parallel"),
        name="attn_prompt",
    )(bias, q, k, v, _tri_ones(ATT_TK))


def _attn_sample_kernel(pt_ref, q_ref, kn_ref, vn_ref, brow_ref, tri_ref, *rest):
    n = PAGES_PER_STEP
    k_refs, v_refs = rest[:n], rest[n:2 * n]
    o_ref, qexp_ref, acc_ref, c_ref = rest[2 * n:]
    s = pl.program_id(1)
    rows = ATT_HEADS * q_ref.shape[0]
    lq = q_ref.shape[0]
    col_head = lax.broadcasted_iota(jnp.int32, (1, ATT_WIDTH), 1) // ATT_HEAD_DIM
    tri = tri_ref[...]
    brow = brow_ref[...]

    def block(kf, vf, mask, token_major):
        kb, vb = kf.astype(BF16), vf.astype(BF16)
        z = (_dot_nt(qexp_ref[...], kb) if token_major else _dot(qexp_ref[...], kb)) + brow
        p, c_new = _stick_block(z, c_ref[...], tri, mask)
        c_ref[...] = c_new
        p = p.astype(BF16)
        return _dot(p, vb) if token_major else _dot_nt(p, vb)

    @pl.when(s == 0)
    def _():
        qt = jnp.concatenate([q_ref[...]] * ATT_HEADS, axis=0)
        row_head = lax.broadcasted_iota(jnp.int32, (rows, 1), 0) // lq
        qexp_ref[...] = jnp.where(row_head == col_head, qt, 0.0).astype(BF16)
        c_ref[...] = jnp.zeros_like(c_ref)
        pad = jnp.zeros((PAGE_SIZE - lq, ATT_WIDTH), F32)
        kn = jnp.concatenate([kn_ref[...], pad], axis=0)
        vn = jnp.concatenate([vn_ref[...], pad], axis=0)
        t = lax.broadcasted_iota(jnp.int32, (rows, PAGE_SIZE), 0) % lq
        tok = lax.broadcasted_iota(jnp.int32, (rows, PAGE_SIZE), 1)
        acc_ref[...] = block(kn, vn, tok < t, True)

    pv = None
    for i in range(n):
        d = block(k_refs[i][...], v_refs[i][...], None, False)
        pv = d if pv is None else pv + d
    acc_ref[...] += pv

    @pl.when(s == pl.num_programs(1) - 1)
    def _():
        out = jnp.zeros((lq, ATT_WIDTH), F32)
        for h in range(ATT_HEADS):
            out = jnp.where(col_head == h, acc_ref[h * lq:(h + 1) * lq, :], out)
        o_ref[...] = out


def _attn_sample(page_table, bias, q, k_new, v_new, cache_k, cache_v):
    db, lq, _ = q.shape
    n_pages = page_table.shape[1]
    n = PAGES_PER_STEP
    rows = ATT_HEADS * lq
    pool = cache_k.shape[0]
    ck = jnp.transpose(cache_k, (0, 2, 3, 1)).reshape(pool, ATT_WIDTH, PAGE_SIZE)
    cv = jnp.transpose(cache_v, (0, 2, 3, 1)).reshape(pool, ATT_WIDTH, PAGE_SIZE)
    brow = jnp.broadcast_to(jnp.repeat(bias.astype(F32), lq)[:, None], (rows, LANES))
    seq = pl.BlockSpec((None, lq, ATT_WIDTH), lambda b, s, pt: (b, 0, 0))

    def page(i):
        return pl.BlockSpec((None, ATT_WIDTH, PAGE_SIZE),
                            lambda b, s, pt: (pt[b, n_pages - 1 - (s * n + i)], 0, 0))

    grid_spec = pltpu.PrefetchScalarGridSpec(
        num_scalar_prefetch=1,
        grid=(db, n_pages // n),
        in_specs=[seq, seq, seq,
                  pl.BlockSpec((rows, LANES), lambda b, s, pt: (0, 0)),
                  pl.BlockSpec((PAGE_SIZE, PAGE_SIZE + LANES), lambda b, s, pt: (0, 0))]
                 + [page(i) for i in range(n)] * 2,
        out_specs=seq,
        scratch_shapes=[pltpu.VMEM((rows, ATT_WIDTH), BF16),
                        pltpu.VMEM((rows, ATT_WIDTH), F32),
                        pltpu.VMEM((rows, LANES), F32)],
    )
    return pl.pallas_call(
        _attn_sample_kernel,
        grid_spec=grid_spec,
        out_shape=jax.ShapeDtypeStruct((db, lq, ATT_WIDTH), F32),
        compiler_params=_params("parallel", "arbitrary"),
        name="attn_sample",
    )(page_table, q, k_new, v_new, brow, _tri_ones(PAGE_SIZE), *([ck] * n), *([cv] * n))


def _ssd_kernel(xbc_ref, z_ref, dtr_ref, s0_ref, c0_ref, wconv_ref, bconv_ref, dtb_ref, alog_ref,
                dskip_ref, gn_ref, tril_ref, e_ref,
                y_ref, sn_ref, cn_ref, cbuf_ref, state_ref):
    c = pl.program_id(1)
    cs = xbc_ref.shape[0]
    ck = SSD_CHUNK
    tail = 8
    hist = SSD_CONV - 1

    @pl.when(c == 0)
    def _():
        state_ref[...] = s0_ref[...]
        cbuf_ref[0:tail, :] = jnp.zeros((tail, CONV_DIM), F32)
        cbuf_ref[tail - hist:tail, :] = c0_ref[...]

    cbuf_ref[tail:tail + cs, :] = xbc_ref[...]
    w = wconv_ref[...]
    xc = bconv_ref[...] + cbuf_ref[tail - hist:tail - hist + cs, :] * w[0:1]
    for j in range(1, SSD_CONV):
        xc = xc + cbuf_ref[tail - hist + j:tail - hist + j + cs, :] * w[j:j + 1]
    cn_ref[...] = cbuf_ref[tail + cs - hist:tail + cs, :]
    last = cbuf_ref[cs:cs + tail, :]
    cbuf_ref[0:tail, :] = last
    xc = _silu(xc)
    dt = _softplus(dtr_ref[...] + dtb_ref[...])
    zg = z_ref[...]
    if cs < ck:
        xc = jnp.concatenate([xc, jnp.zeros((ck - cs, CONV_DIM), F32)], axis=0)
        dt = jnp.concatenate([dt, jnp.zeros((ck - cs, LANES), F32)], axis=0)
        zg = jnp.concatenate([zg, jnp.zeros((ck - cs, SSD_WIDTH), F32)], axis=0)

    a = -jnp.exp(alog_ref[...])
    acum = _dot_exact_rhs(tril_ref[...], dt * a)
    acum_row = acum.T
    e = e_ref[...]
    dt_x = _dot_exact_lhs(dt, e)
    ac_x = _dot_exact_lhs(acum, e)
    xs = xc[:, :SSD_WIDTH]
    xd = xs * dt_x
    xdw = (xd * jnp.exp(ac_x[ck - 1:ck, :] - ac_x)).astype(BF16)
    eac_x = jnp.exp(ac_x)
    row_i = lax.broadcasted_iota(jnp.int32, (ck, ck), 0)
    col_j = lax.broadcasted_iota(jnp.int32, (ck, ck), 1)
    causal = row_i >= col_j
    low = lax.broadcasted_iota(jnp.int32, (1, LANES), 1) < SSD_HEAD_DIM
    per_group = SSD_HEADS // SSD_GROUPS
    ys = []
    gmat = {}
    for hp in range(SSD_WIDTH // LANES):
        g = (2 * hp) // per_group
        b_g = xc[:, SSD_WIDTH + g * SSD_STATE:SSD_WIDTH + (g + 1) * SSD_STATE].astype(BF16)
        c_off = SSD_WIDTH + SSD_GROUPS * SSD_STATE
        c_g = xc[:, c_off + g * SSD_STATE:c_off + (g + 1) * SSD_STATE].astype(BF16)
        if g not in gmat:
            gmat[g] = _dot_nt(c_g, b_g)
        sl = slice(hp * LANES, (hp + 1) * LANES)
        xd_p = xd[:, sl]
        y_p = None
        decs = []
        for k in range(2):
            h = 2 * hp + k
            diff = acum[:, h:h + 1] - acum_row[h:h + 1, :]
            decay = jnp.where(causal, jnp.exp(jnp.minimum(diff, 0.0)), 0.0)
            sc = (gmat[g] * decay).astype(BF16)
            xd_k = jnp.where(low if k == 0 else jnp.logical_not(low), xd_p, 0.0).astype(BF16)
            d = _dot(sc, xd_k)
            y_p = d if y_p is None else y_p + d
            decs.append(jnp.broadcast_to(jnp.exp(acum[ck - 1:ck, h:h + 1]), (SSD_HEAD_DIM, SSD_STATE)))
        s_p = state_ref[hp]
        y_p = y_p + _dot_nt(c_g, s_p.astype(BF16)) * eac_x[:, sl]
        state_ref[hp] = s_p * jnp.concatenate(decs, axis=0) + _dot_tn(xdw[:, sl], b_g)
        ys.append(y_p + dskip_ref[:, sl] * xs[:, sl])
    y = jnp.concatenate(ys, axis=1) * _silu(zg)
    y = _rms(y) * gn_ref[...]
    y_ref[...] = y[:cs, :]

    @pl.when(c == pl.num_programs(1) - 1)
    def _():
        sn_ref[...] = state_ref[...]


def _ssd(xbc, z, dtr, s0, c0, wconv, bconv, dtb, alog, dskip, gn, batch):
    rows = xbc.shape[0]
    l = rows // batch
    cs = min(SSD_CHUNK, l)
    nc = l // cs
    pairs = SSD_WIDTH // LANES
    s0 = s0.reshape(batch, pairs, 2 * SSD_HEAD_DIM, SSD_STATE)
    i = lax.broadcasted_iota(jnp.int32, (SSD_CHUNK, SSD_CHUNK), 0)
    j = lax.broadcasted_iota(jnp.int32, (SSD_CHUNK, SSD_CHUNK), 1)
    tril = (j <= i).astype(BF16)
    eh = lax.broadcasted_iota(jnp.int32, (LANES, SSD_WIDTH), 0)
    ec = lax.broadcasted_iota(jnp.int32, (LANES, SSD_WIDTH), 1) // SSD_HEAD_DIM
    expand = (eh == ec).astype(BF16)
    row = lambda n: pl.BlockSpec((cs, n), lambda b, c: (b * nc + c, 0))
    vec = lambda a: pl.BlockSpec(a.shape, lambda b, c: (0, 0))
    st = pl.BlockSpec((None, pairs, 2 * SSD_HEAD_DIM, SSD_STATE), lambda b, c: (b, 0, 0, 0))
    cv = pl.BlockSpec((None, SSD_CONV - 1, CONV_DIM), lambda b, c: (b, 0, 0))
    y, sn, cn = pl.pallas_call(
        _ssd_kernel,
        grid=(batch, nc),
        in_specs=[row(CONV_DIM), row(SSD_WIDTH), row(LANES), st, cv,
                  vec(wconv), vec(bconv), vec(dtb), vec(alog), vec(dskip), vec(gn), vec(tril), vec(expand)],
        out_specs=(row(SSD_WIDTH), st, cv),
        out_shape=(jax.ShapeDtypeStruct((rows, SSD_WIDTH), F32),
                   jax.ShapeDtypeStruct(s0.shape, F32),
                   jax.ShapeDtypeStruct((batch, SSD_CONV - 1, CONV_DIM), F32)),
        scratch_shapes=[pltpu.VMEM((8 + cs, CONV_DIM), F32),
                        pltpu.VMEM((pairs, 2 * SSD_HEAD_DIM, SSD_STATE), F32)],
        compiler_params=_params("parallel", "arbitrary"),
        name="ssd",
    )(xbc, z, dtr, s0, c0, wconv, bconv, dtb, alog, dskip, gn, tril, expand)
    return y, sn.reshape(batch, SSD_HEADS, SSD_HEAD_DIM, SSD_STATE), cn


def _out_proj_kernel(o_ref, y_ref, x_ref, woa_ref, wob_ref, gpost_ref, gpre_ref, h_ref, hn_ref):
    mix = _dot(o_ref[...].astype(BF16), woa_ref[...]) + _dot(y_ref[...].astype(BF16), wob_ref[...])
    h = x_ref[...] + _rms(mix) * gpost_ref[...]
    h_ref[...] = h
    hn_ref[...] = (_rms(h) * gpre_ref[...]).astype(BF16)


def _out_proj(o, y, x, woa, wob, gpost, gpre):
    rows, d = x.shape
    tm = min(ROW_TILE, rows)
    row = lambda n: pl.BlockSpec((tm, n), lambda i: (i, 0))
    return pl.pallas_call(
        _out_proj_kernel,
        grid=(rows // tm,),
        in_specs=[row(ATT_WIDTH), row(SSD_WIDTH), row(d),
                  _const_spec(woa.shape), _const_spec(wob.shape), _const_spec(gpost.shape), _const_spec(gpre.shape)],
        out_specs=(row(d), row(d)),
        out_shape=(jax.ShapeDtypeStruct((rows, d), F32), jax.ShapeDtypeStruct((rows, d), BF16)),
        compiler_params=_params("parallel"),
        name="out_proj",
    )(o, y, x, woa, wob, gpost, gpre)


def _gelu_tanh(x):
    return 0.5 * x * (1.0 + jnp.tanh(0.7978845608028654 * (x + 0.044715 * (x * x * x))))


def _ffn_kernel(*refs, seq_len):
    if seq_len is None:
        hn_ref, halo_ref, wg_ref, wv_ref, cwg_ref, cwv_ref, cbg_ref, cbv_ref, wd_ref, f_ref, ug_ref, uv_ref, xe_ref = refs
    else:
        hn_ref, stg_ref, stv_ref, wg_ref, wv_ref, cwg_ref, cwv_ref, cbg_ref, cbv_ref, wd_ref, f_ref, ug_ref, uv_ref = refs
    i = pl.program_id(0)
    j = pl.program_id(1)
    tm = hn_ref.shape[0]

    if seq_len is None:
        @pl.when(j == 0)
        def _():
            halo = halo_ref[...]
            xe_ref[0:FFN_HALO, :] = jnp.where(i > 0, halo, jnp.zeros_like(halo))
            xe_ref[FFN_HALO:, :] = hn_ref[...]

        def conv(w_ref, cw_ref, cb_ref, u_ref):
            up = _dot(xe_ref[...], w_ref[...])
            cw = cw_ref[...]
            y = (cb_ref[...] + pltpu.roll(up, 2, 0)[FFN_HALO:] * cw[0:1]
                 + pltpu.roll(up, 1, 0)[FFN_HALO:] * cw[1:2] + up[FFN_HALO:] * cw[2:3])
            u_ref[...] = up[FFN_HALO + tm - 8:]
            return y
        gate = conv(wg_ref, cwg_ref, cbg_ref, ug_ref)
        val = conv(wv_ref, cwv_ref, cbv_ref, uv_ref)
    else:
        t = lax.broadcasted_iota(jnp.int32, (tm, 1), 0) % seq_len

        def conv(w_ref, cw_ref, cb_ref, st_ref, u_ref):
            up = _dot(hn_ref[...], w_ref[...])
            st = st_ref[...]
            cw = cw_ref[...]
            prev2 = jnp.where(t < 2, st, pltpu.roll(up, 2, 0))
            prev1 = jnp.where(t < 1, pltpu.roll(st, tm - 1, 0), pltpu.roll(up, 1, 0))
            u_ref[...] = up
            return cb_ref[...] + prev2 * cw[0:1] + prev1 * cw[1:2] + up * cw[2:3]
        gate = conv(wg_ref, cwg_ref, cbg_ref, stg_ref, ug_ref)
        val = conv(wv_ref, cwv_ref, cbv_ref, stv_ref, uv_ref)

    d = _dot((_gelu_tanh(gate) * val).astype(BF16), wd_ref[...])

    @pl.when(j == 0)
    def _():
        f_ref[...] = d

    @pl.when(j > 0)
    def _():
        f_ref[...] += d


def _ffn(hn, w_up, cw, cb, w_down, state=None, seq_len=None):
    rows, d = hn.shape
    dff = w_down.shape[0]
    tf = FFN_TF
    nj = dff // tf
    tm = min(ROW_TILE, rows)
    gcol = lambda r: pl.BlockSpec((r, tf), lambda i, j: (0, j))
    vcol = lambda r: pl.BlockSpec((r, tf), lambda i, j: (0, j + nj))
    hn_spec = pl.BlockSpec((tm, d), lambda i, j: (i, 0))
    common = [pl.BlockSpec((d, tf), lambda i, j: (0, j)), pl.BlockSpec((d, tf), lambda i, j: (0, j + nj)),
              gcol(FFN_CONV), vcol(FFN_CONV), gcol(1), vcol(1),
              pl.BlockSpec((tf, d), lambda i, j: (j, 0))]
    common_args = (w_up, w_up, cw, cw, cb, cb, w_down)
    f_spec = pl.BlockSpec((tm, d), lambda i, j: (i, 0))
    if seq_len is None:
        per = tm // FFN_HALO
        halo = pl.BlockSpec((FFN_HALO, d), lambda i, j: (jnp.maximum(i * per - 1, 0), 0))
        in_specs = [hn_spec, halo] + common
        args = (hn, hn) + common_args
        u_rows = 8
        u_spec = pl.BlockSpec((u_rows, tf), lambda i, j: (0, j))
        scratch = [pltpu.VMEM((FFN_HALO + tm, d), BF16)]
    else:
        assert rows == tm
        in_specs = [hn_spec, pl.BlockSpec((tm, tf), lambda i, j: (0, j)),
                    pl.BlockSpec((tm, tf), lambda i, j: (0, j + nj))] + common
        args = (hn, state, state) + common_args
        u_rows = tm
        u_spec = pl.BlockSpec((u_rows, tf), lambda i, j: (0, j))
        scratch = []
    return pl.pallas_call(
        functools.partial(_ffn_kernel, seq_len=seq_len),
        grid=(rows // tm, nj),
        in_specs=in_specs,
        out_specs=(f_spec, u_spec, u_spec),
        out_shape=(jax.ShapeDtypeStruct((rows, d), F32),
                   jax.ShapeDtypeStruct((u_rows, dff), F32),
                   jax.ShapeDtypeStruct((u_rows, dff), F32)),
        scratch_shapes=scratch,
        compiler_params=_params("arbitrary", "arbitrary"),
        name="ffn",
    )(*args)


def _ple_kernel(h_ref, f_ref, p_ref, gpost_ref, wg_ref, wp_ref, o_ref):
    h = h_ref[...] + _rms(f_ref[...]) * gpost_ref[...]
    gate = 1.0 / (1.0 + jnp.exp(-_dot(h.astype(BF16), wg_ref[...])))
    o_ref[...] = h + _dot(p_ref[...].astype(BF16), wp_ref[...]) * gate


def _ple(h, f, p, gpost, wg, wp):
    rows, d = h.shape
    tm = min(ROW_TILE, rows)
    row = lambda n: pl.BlockSpec((tm, n), lambda i: (i, 0))
    return pl.pallas_call(
        _ple_kernel,
        grid=(rows // tm,),
        in_specs=[row(d), row(d), row(p.shape[1]),
                  _const_spec(gpost.shape), _const_spec(wg.shape), _const_spec(wp.shape)],
        out_specs=row(d),
        out_shape=jax.ShapeDtypeStruct((rows, d), F32),
        compiler_params=_params("parallel"),
        name="ple",
    )(h, f, p, gpost, wg, wp)


def _layer(x, p, lw, batch, ssm0, conv0, ffn0, past):
    rows, d = x.shape
    l = rows // batch
    sample = past is not None
    q, kf, vf, kb, vb, z, xbc, dtr = _in_proj(x, lw["g_pre_mix"], lw["w_qkv"], lw["w_z"], lw["w_xbc"], lw["w_dt"],
                                                F32 if sample else BF16)
    if sample:
        page_table, cache_k, cache_v = past
        o = _attn_sample(page_table, lw["sb_bias"], q.reshape(batch, l, ATT_WIDTH), kf.reshape(batch, l, ATT_WIDTH),
                         vf.reshape(batch, l, ATT_WIDTH), cache_k, cache_v).reshape(rows, ATT_WIDTH)
    else:
        o = _attn_prompt(lw["sb_bias"], q, kb, vb)
    y, ssm_new, conv_new = _ssd(xbc, z, dtr, ssm0, conv0, lw["w_conv"], lw["b_conv"], lw["dt_bias"], lw["a_log"],
                                lw["d_skip"], lw["g_ssd_norm"], batch)
    h1, hn = _out_proj(o, y, x, lw["w_out_a"], lw["w_out_b"], lw["g_post_mix"], lw["g_pre_ffn"])
    dff2 = lw["w_up"].shape[1]
    if sample:
        st = jnp.pad(ffn0, ((0, 0), (0, l - (FFN_CONV - 1)), (0, 0))).reshape(rows, dff2)
        f, ug, uv = _ffn(hn, lw["w_up"], lw["w_ffn_conv"], lw["b_ffn_conv"], lw["w_down"], state=st, seq_len=l)
        up = jnp.concatenate([ug, uv], axis=1).reshape(batch, l, dff2)
        ffn_new = up[:, l - (FFN_CONV - 1):, :]
    else:
        f, ug, uv = _ffn(hn, lw["w_up"], lw["w_ffn_conv"], lw["b_ffn_conv"], lw["w_down"])
        ffn_new = jnp.concatenate([ug, uv], axis=1)[None, 8 - (FFN_CONV - 1):, :]
    out = _ple(h1, f, p, lw["g_post_ffn"], lw["w_ple_gate"], lw["w_ple_proj"])
    return out, (kf, vf, ssm_new, conv_new, ffn_new)


def _prep_weights(i, g_pre_mix, w_in, sb_bias, w_conv, b_conv, dt_bias, a_log, d_skip, g_ssd_norm, w_out,
                  g_post_mix, g_pre_ffn, w_up, w_ffn_conv, b_ffn_conv, w_down, g_post_ffn, w_ple_gate, w_ple_proj):
    row = lambda a: a[i][None, :].astype(F32)
    lane_pad = lambda a: jnp.pad(a[i].astype(F32), (0, LANES - a.shape[1]))[None, :]
    w = w_in[i].astype(BF16)
    o_z = 3 * ATT_WIDTH
    o_x = o_z + SSD_WIDTH
    o_dt = o_x + CONV_DIM
    return {
        "g_pre_mix": row(g_pre_mix),
        "w_qkv": w[:, :o_z], "w_z": w[:, o_z:o_x], "w_xbc": w[:, o_x:o_dt],
        "w_dt": jnp.pad(w[:, o_dt:], ((0, 0), (0, LANES - SSD_HEADS))),
        "sb_bias": sb_bias[i].astype(F32),
        "w_conv": w_conv[i].astype(F32), "b_conv": row(b_conv),
        "dt_bias": lane_pad(dt_bias), "a_log": lane_pad(a_log),
        "d_skip": jnp.repeat(d_skip[i].astype(F32), SSD_HEAD_DIM)[None, :],
        "g_ssd_norm": row(g_ssd_norm),
        "w_out_a": w_out[i, :ATT_WIDTH].astype(BF16), "w_out_b": w_out[i, ATT_WIDTH:].astype(BF16),
        "g_post_mix": row(g_post_mix), "g_pre_ffn": row(g_pre_ffn),
        "w_up": w_up[i].astype(BF16), "w_ffn_conv": w_ffn_conv[i].astype(F32), "b_ffn_conv": row(b_ffn_conv),
        "w_down": w_down[i].astype(BF16), "g_post_ffn": row(g_post_ffn),
        "w_ple_gate": w_ple_gate[i].astype(BF16), "w_ple_proj": w_ple_proj[i].astype(BF16),
    }


def kernel(x_prompt, x_sample, cache_k, cache_v, state_ssm, state_conv, state_ffn_conv, page_table, p_prompt, p_sample, g_pre_mix, w_in, sb_bias, w_conv, b_conv, dt_bias, a_log, d_skip, g_ssd_norm, w_out, g_post_mix, g_pre_ffn, w_up, w_ffn_conv, b_ffn_conv, w_down, g_post_ffn, w_ple_gate, w_ple_proj):
    bp, lp, d = x_prompt.shape
    db, ls, _ = x_sample.shape
    depth = w_in.shape[0]
    dff2 = w_up.shape[2]
    hp = x_prompt.reshape(bp * lp, d)
    hs = x_sample.reshape(db * ls, d)
    outs_p, outs_s = [], []
    for i in range(depth):
        lw = _prep_weights(i, g_pre_mix, w_in, sb_bias, w_conv, b_conv, dt_bias, a_log, d_skip, g_ssd_norm, w_out,
                           g_post_mix, g_pre_ffn, w_up, w_ffn_conv, b_ffn_conv, w_down, g_post_ffn,
                           w_ple_gate, w_ple_proj)
        ssm0 = jnp.zeros((bp, SSD_HEADS, SSD_HEAD_DIM, SSD_STATE), F32)
        conv0 = jnp.zeros((bp, SSD_CONV - 1, CONV_DIM), F32)
        hp, st_p = _layer(hp, p_prompt[i].reshape(bp * lp, -1), lw, bp, ssm0, conv0, None, None)
        hs, st_s = _layer(hs, p_sample[i].reshape(db * ls, -1), lw, db, state_ssm[i], state_conv[i],
                          state_ffn_conv[i], (page_table, cache_k[i], cache_v[i]))
        outs_p.append(st_p)
        outs_s.append(st_s)

    def stack(outs, j, shape):
        return jnp.stack([o[j].reshape(shape) for o in outs], axis=0)

    kv_p = (bp, lp, ATT_HEADS, ATT_HEAD_DIM)
    kv_s = (db, ls, ATT_HEADS, ATT_HEAD_DIM)
    ssm_p = (bp, SSD_HEADS, SSD_HEAD_DIM, SSD_STATE)
    ssm_s = (db, SSD_HEADS, SSD_HEAD_DIM, SSD_STATE)
    return (hp.reshape(bp, lp, d), hs.reshape(db, ls, d),
            stack(outs_p, 0, kv_p), stack(outs_s, 0, kv_s), stack(outs_p, 1, kv_p), stack(outs_s, 1, kv_s),
            stack(outs_p, 2, ssm_p), stack(outs_s, 2, ssm_s),
            stack(outs_p, 3, (bp, SSD_CONV - 1, CONV_DIM)), stack(outs_s, 3, (db, SSD_CONV - 1, CONV_DIM)),
            stack(outs_p, 4, (bp, FFN_CONV - 1, dff2)), stack(outs_s, 4, (db, FFN_CONV - 1, dff2)))
```

```python
import functools

import jax
import jax.numpy as jnp
from jax import lax
from jax.experimental import pallas as pl
from jax.experimental.pallas import tpu as pltpu

F32 = jnp.float32
BF16 = jnp.bfloat16

ATT_HEADS = 16
ATT_HEAD_DIM = 64
ATT_WIDTH = ATT_HEADS * ATT_HEAD_DIM
SSD_HEADS = 16
SSD_HEAD_DIM = 64
SSD_WIDTH = SSD_HEADS * SSD_HEAD_DIM
SSD_GROUPS = 2
SSD_STATE = 128
SSD_CONV = 4
SSD_CHUNK = 128
CONV_DIM = SSD_WIDTH + 2 * SSD_GROUPS * SSD_STATE
FFN_CONV = 3
RMS_EPS = 1e-6
PAGE_SIZE = 128

LANES = 128
HEAD_PAIRS = ATT_WIDTH // LANES
VMEM_LIMIT = 56 * 1024 * 1024

ROW_TILE = 512
IN_PROJ_TILE = 256
ATT_TQ = 512
ATT_TK = 256
FFN_TF = 512
FFN_HALO = 16
PAGES_PER_STEP = 8


def _params(*sem):
    return pltpu.CompilerParams(dimension_semantics=sem, vmem_limit_bytes=VMEM_LIMIT)


def _rms(x):
    return x * lax.rsqrt(jnp.mean(x * x, axis=-1, keepdims=True) + RMS_EPS)


def _softplus(x):
    return jnp.maximum(x, 0.0) + jnp.log(1.0 + jnp.exp(-jnp.abs(x)))


def _silu(x):
    return x * (1.0 / (1.0 + jnp.exp(-x)))


def _dot(a, b):
    return jnp.dot(a, b, preferred_element_type=F32)


def _dot_nt(a, b):
    return lax.dot_general(a, b, (((1,), (1,)), ((), ())), preferred_element_type=F32)


def _dot_tn(a, b):
    return lax.dot_general(a, b, (((0,), (0,)), ((), ())), preferred_element_type=F32)


def _split3(x):
    hi = x.astype(BF16)
    r1 = x - hi.astype(F32)
    mid = r1.astype(BF16)
    lo = (r1 - mid.astype(F32)).astype(BF16)
    return hi, mid, lo


def _dot_exact_rhs(m, x):
    hi, mid, lo = _split3(x)
    return _dot(m, hi) + _dot(m, mid) + _dot(m, lo)


def _dot_exact_lhs(x, m):
    hi, mid, lo = _split3(x)
    return _dot(hi, m) + _dot(mid, m) + _dot(lo, m)


def _const_spec(shape):
    return pl.BlockSpec(shape, lambda *_: (0,) * len(shape))


def _in_proj_kernel(x_ref, g_ref, wqkv_ref, wz_ref, wxbc_ref, wdt_ref,
                    q_ref, kf_ref, vf_ref, kb_ref, vb_ref, z_ref, xbc_ref, dt_ref):
    u = (_rms(x_ref[...]) * g_ref[...]).astype(BF16)
    qkv = _dot(u, wqkv_ref[...])
    q_ref[...] = (qkv[:, :ATT_WIDTH] * (ATT_HEAD_DIM ** -0.5)).astype(q_ref.dtype)
    k = qkv[:, ATT_WIDTH:2 * ATT_WIDTH]
    v = qkv[:, 2 * ATT_WIDTH:]
    kf_ref[...] = k
    vf_ref[...] = v
    kb_ref[...] = k.astype(BF16)
    vb_ref[...] = v.astype(BF16)
    z_ref[...] = _dot(u, wz_ref[...])
    xbc_ref[...] = _dot(u, wxbc_ref[...])
    dt_ref[...] = _dot(u, wdt_ref[...])


def _in_proj(x, g, wqkv, wz, wxbc, wdt, q_dtype):
    rows, d = x.shape
    tm = min(IN_PROJ_TILE, rows)
    row = lambda n: pl.BlockSpec((tm, n), lambda i: (i, 0))
    res = lambda a: pl.BlockSpec(a.shape, lambda i: (0, 0), pipeline_mode=pl.Buffered(1))
    out_shape = (
        jax.ShapeDtypeStruct((rows, ATT_WIDTH), q_dtype),
        jax.ShapeDtypeStruct((rows, ATT_WIDTH), F32),
        jax.ShapeDtypeStruct((rows, ATT_WIDTH), F32),
        jax.ShapeDtypeStruct((rows, ATT_WIDTH), BF16),
        jax.ShapeDtypeStruct((rows, ATT_WIDTH), BF16),
        jax.ShapeDtypeStruct((rows, SSD_WIDTH), F32),
        jax.ShapeDtypeStruct((rows, CONV_DIM), F32),
        jax.ShapeDtypeStruct((rows, LANES), F32),
    )
    return pl.pallas_call(
        _in_proj_kernel,
        grid=(rows // tm,),
        in_specs=[row(d), res(g), res(wqkv), res(wz), res(wxbc), res(wdt)],
        out_specs=tuple(row(s.shape[1]) for s in out_shape),
        out_shape=out_shape,
        compiler_params=_params("parallel"),
        name="in_proj",
    )(x, g, wqkv, wz, wxbc, wdt)


def _stick_block(z, carry, tri, mask):
    rows, tk = z.shape
    sp = _softplus(z)
    if mask is not None:
        sp = jnp.where(mask, sp, 0.0)
    cs = _dot(sp.astype(BF16), tri)
    wide = carry if tk == LANES else jnp.concatenate([carry] * (tk // LANES), axis=1)
    p = jnp.exp(z - cs[:, :tk] - wide)
    if mask is not None:
        p = jnp.where(mask, p, 0.0)
    total = cs[:, tk:] if tri.shape[1] > tk else jnp.broadcast_to(cs[:, 0:1], (rows, LANES))
    return p, carry + total


def _tri(tk, with_ones):
    n = tk + LANES if with_ones else tk
    s = lax.broadcasted_iota(jnp.int32, (tk, n), 0)
    j = lax.broadcasted_iota(jnp.int32, (tk, n), 1)
    return ((s >= j) | (j >= tk)).astype(BF16)


def _attn_prompt_kernel(bias_ref, q_ref, k_ref, v_ref, tri_ref, o_ref, acc_ref, c_ref):
    hp = pl.program_id(0)
    qi = pl.program_id(1)
    tq, tk = ATT_TQ, ATT_TK
    per = tq // tk
    low = lax.broadcasted_iota(jnp.int32, (1, LANES), 1) < ATT_HEAD_DIM
    q = q_ref[...]
    qz = jnp.zeros_like(q)
    qs = (jnp.where(low, q, qz), jnp.where(low, qz, q))
    bias = (bias_ref[2 * hp], bias_ref[2 * hp + 1])
    tri = tri_ref[...]
    acc_ref[...] = jnp.zeros_like(acc_ref)
    c_ref[...] = jnp.zeros_like(c_ref)
    q_pos = qi * tq + lax.broadcasted_iota(jnp.int32, (tq, tk), 0)
    k_off = lax.broadcasted_iota(jnp.int32, (tq, tk), 1)

    def block(j, masked):
        start = pl.multiple_of(j * tk, tk)
        kb = k_ref[pl.ds(start, tk), :]
        vb = v_ref[pl.ds(start, tk), :]
        vz = jnp.zeros_like(vb)
        vs = (jnp.where(low, vb, vz), jnp.where(low, vz, vb))
        mask = (j * tk + k_off) < q_pos if masked else None
        pv = None
        for a in range(2):
            z = _dot_nt(qs[a], kb) + bias[a]
            p, c_new = _stick_block(z, c_ref[a], tri, mask)
            c_ref[a] = c_new
            d = _dot(p.astype(BF16), vs[a])
            pv = d if pv is None else pv + d
        acc_ref[...] += pv

    def diagonal(r, carry):
        block(qi * per + per - 1 - r, True)
        return carry

    def body(jj, carry):
        block(qi * per - 1 - jj, False)
        return carry

    lax.fori_loop(0, per, diagonal, 0)
    lax.fori_loop(0, qi * per, body, 0)
    o_ref[...] = acc_ref[...].astype(o_ref.dtype)


def _attn_prompt(bias, q, k, v):
    l = q.shape[0]
    tq = ATT_TQ
    grid_spec = pltpu.PrefetchScalarGridSpec(
        num_scalar_prefetch=1,
        grid=(HEAD_PAIRS, l // tq),
        in_specs=[
            pl.BlockSpec((tq, LANES), lambda hp, qi, b: (qi, hp)),
            pl.BlockSpec((l, LANES), lambda hp, qi, b: (0, hp)),
            pl.BlockSpec((l, LANES), lambda hp, qi, b: (0, hp)),
            pl.BlockSpec((ATT_TK, ATT_TK), lambda hp, qi, b: (0, 0)),
        ],
        out_specs=pl.BlockSpec((tq, LANES), lambda hp, qi, b: (qi, hp)),
        scratch_shapes=[pltpu.VMEM((tq, LANES), F32), pltpu.VMEM((2, tq, LANES), F32)],
    )
    return pl.pallas_call(
        _attn_prompt_kernel,
        grid_spec=grid_spec,
        out_shape=jax.ShapeDtypeStruct((l, ATT_WIDTH), BF16),
        compiler_params=_params("parallel", "parallel"),
        name="attn_prompt",
    )(bias, q, k, v, _tri(ATT_TK, False))


def _attn_sample_kernel(pt_ref, q_ref, kn_ref, vn_ref, brow_ref, tri_ref, *rest):
    n = PAGES_PER_STEP
    k_refs, v_refs = rest[:n], rest[n:2 * n]
    o_ref, qexp_ref, acc_ref, c_ref = rest[2 * n:]
    s = pl.program_id(1)
    rows = ATT_HEADS * q_ref.shape[0]
    lq = q_ref.shape[0]
    col_head = lax.broadcasted_iota(jnp.int32, (1, ATT_WIDTH), 1) // ATT_HEAD_DIM
    tri = tri_ref[...]
    brow = brow_ref[...]

    @pl.when(s == 0)
    def _():
        qt = jnp.concatenate([q_ref[...]] * ATT_HEADS, axis=0)
        row_head = lax.broadcasted_iota(jnp.int32, (rows, 1), 0) // lq
        qexp_ref[...] = jnp.where(row_head == col_head, qt, 0.0).astype(BF16)
        c_ref[...] = jnp.zeros_like(c_ref)
        pad = jnp.zeros((PAGE_SIZE - lq, ATT_WIDTH), F32)
        kn = jnp.concatenate([kn_ref[...], pad], axis=0)
        vn = jnp.concatenate([vn_ref[...], pad], axis=0)
        t = lax.broadcasted_iota(jnp.int32, (rows, PAGE_SIZE), 0) % lq
        tok = lax.broadcasted_iota(jnp.int32, (rows, PAGE_SIZE), 1)
        z = _dot_nt(qexp_ref[...], kn.astype(BF16)) + brow
        p, c_ref[...] = _stick_block(z, jnp.zeros_like(brow), tri, tok < t)
        acc_ref[...] = _dot(p.astype(BF16), vn.astype(BF16))

    kcat = jnp.concatenate([k_refs[i][...].astype(BF16) for i in reversed(range(n))], axis=1)
    vcat = jnp.concatenate([v_refs[i][...].astype(BF16) for i in reversed(range(n))], axis=1)
    z = _dot(qexp_ref[...], kcat)
    c = c_ref[...]
    ps = [None] * n
    for i in reversed(range(n)):
        zi = z[:, i * PAGE_SIZE:(i + 1) * PAGE_SIZE] + brow
        p, c = _stick_block(zi, c, tri, None)
        ps[i] = p.astype(BF16)
    c_ref[...] = c
    acc_ref[...] += _dot_nt(jnp.concatenate(ps, axis=1), vcat)

    @pl.when(s == pl.num_programs(1) - 1)
    def _():
        out = jnp.zeros((lq, ATT_WIDTH), F32)
        for h in range(ATT_HEADS):
            out = jnp.where(col_head == h, acc_ref[h * lq:(h + 1) * lq, :], out)
        o_ref[...] = out


def _attn_sample(page_table, bias, q, k_new, v_new, cache_k, cache_v):
    db, lq, _ = q.shape
    n_pages = page_table.shape[1]
    n = PAGES_PER_STEP
    rows = ATT_HEADS * lq
    pool = cache_k.shape[0]
    ck = jnp.transpose(cache_k, (0, 2, 3, 1)).reshape(pool, ATT_WIDTH, PAGE_SIZE)
    cv = jnp.transpose(cache_v, (0, 2, 3, 1)).reshape(pool, ATT_WIDTH, PAGE_SIZE)
    brow = jnp.broadcast_to(jnp.repeat(bias.astype(F32), lq)[:, None], (rows, LANES))
    seq = pl.BlockSpec((None, lq, ATT_WIDTH), lambda b, s, pt: (b, 0, 0))

    def page(i):
        return pl.BlockSpec((None, ATT_WIDTH, PAGE_SIZE),
                            lambda b, s, pt: (pt[b, n_pages - 1 - (s * n + i)], 0, 0))

    grid_spec = pltpu.PrefetchScalarGridSpec(
        num_scalar_prefetch=1,
        grid=(db, n_pages // n),
        in_specs=[seq, seq, seq,
                  pl.BlockSpec((rows, LANES), lambda b, s, pt: (0, 0)),
                  pl.BlockSpec((PAGE_SIZE, PAGE_SIZE + LANES), lambda b, s, pt: (0, 0))]
                 + [page(i) for i in range(n)] * 2,
        out_specs=seq,
        scratch_shapes=[pltpu.VMEM((rows, ATT_WIDTH), BF16),
                        pltpu.VMEM((rows, ATT_WIDTH), F32),
                        pltpu.VMEM((rows, LANES), F32)],
    )
    return pl.pallas_call(
        _attn_sample_kernel,
        grid_spec=grid_spec,
        out_shape=jax.ShapeDtypeStruct((db, lq, ATT_WIDTH), F32),
        compiler_params=_params("parallel", "arbitrary"),
        name="attn_sample",
    )(page_table, q, k_new, v_new, brow, _tri(PAGE_SIZE, True), *([ck] * n), *([cv] * n))


def _ssd_kernel(xbc_ref, z_ref, dtr_ref, s0_ref, c0_ref, wconv_ref, bconv_ref, dtb_ref, alog_ref,
                dskip_ref, gn_ref, tril_ref, e_ref,
                y_ref, sn_ref, cn_ref, cbuf_ref, state_ref):
    c = pl.program_id(1)
    cs = xbc_ref.shape[0]
    ck = SSD_CHUNK
    tail = 8
    hist = SSD_CONV - 1

    @pl.when(c == 0)
    def _():
        state_ref[...] = s0_ref[...]
        cbuf_ref[0:tail, :] = jnp.zeros((tail, CONV_DIM), F32)
        cbuf_ref[tail - hist:tail, :] = c0_ref[...]

    cbuf_ref[tail:tail + cs, :] = xbc_ref[...]
    w = wconv_ref[...]
    xc = bconv_ref[...] + cbuf_ref[tail - hist:tail - hist + cs, :] * w[0:1]
    for j in range(1, SSD_CONV):
        xc = xc + cbuf_ref[tail - hist + j:tail - hist + j + cs, :] * w[j:j + 1]
    cn_ref[...] = cbuf_ref[tail + cs - hist:tail + cs, :]
    last = cbuf_ref[cs:cs + tail, :]
    cbuf_ref[0:tail, :] = last
    xc = _silu(xc)
    dt = _softplus(dtr_ref[...] + dtb_ref[...])
    zg = z_ref[...]
    if cs < ck:
        xc = jnp.concatenate([xc, jnp.zeros((ck - cs, CONV_DIM), F32)], axis=0)
        dt = jnp.concatenate([dt, jnp.zeros((ck - cs, LANES), F32)], axis=0)
        zg = jnp.concatenate([zg, jnp.zeros((ck - cs, SSD_WIDTH), F32)], axis=0)

    a = -jnp.exp(alog_ref[...])
    acum = _dot_exact_rhs(tril_ref[...], dt * a)
    acum_row = acum.T
    e = e_ref[...]
    dt_x = _dot_exact_lhs(dt, e)
    ac_x = _dot_exact_lhs(acum, e)
    xs = xc[:, :SSD_WIDTH]
    xd = xs * dt_x
    xdw = (xd * jnp.exp(ac_x[ck - 1:ck, :] - ac_x)).astype(BF16)
    eac_x = jnp.exp(ac_x)
    row_i = lax.broadcasted_iota(jnp.int32, (ck, ck), 0)
    col_j = lax.broadcasted_iota(jnp.int32, (ck, ck), 1)
    causal = row_i >= col_j
    low = lax.broadcasted_iota(jnp.int32, (1, LANES), 1) < SSD_HEAD_DIM
    per_group = SSD_HEADS // SSD_GROUPS
    ys = []
    gmat = {}
    for hp in range(SSD_WIDTH // LANES):
        g = (2 * hp) // per_group
        b_g = xc[:, SSD_WIDTH + g * SSD_STATE:SSD_WIDTH + (g + 1) * SSD_STATE].astype(BF16)
        c_off = SSD_WIDTH + SSD_GROUPS * SSD_STATE
        c_g = xc[:, c_off + g * SSD_STATE:c_off + (g + 1) * SSD_STATE].astype(BF16)
        if g not in gmat:
            gmat[g] = _dot_nt(c_g, b_g)
        sl = slice(hp * LANES, (hp + 1) * LANES)
        xd_p = xd[:, sl]
        y_p = None
        decs = []
        for k in range(2):
            h = 2 * hp + k
            diff = acum[:, h:h + 1] - acum_row[h:h + 1, :]
            decay = jnp.where(causal, jnp.exp(jnp.minimum(diff, 0.0)), 0.0)
            sc = (gmat[g] * decay).astype(BF16)
            xd_k = jnp.where(low if k == 0 else jnp.logical_not(low), xd_p, 0.0).astype(BF16)
            d = _dot(sc, xd_k)
            y_p = d if y_p is None else y_p + d
            decs.append(jnp.broadcast_to(jnp.exp(acum[ck - 1:ck, h:h + 1]), (SSD_HEAD_DIM, SSD_STATE)))
        s_p = state_ref[hp]
        y_p = y_p + _dot_nt(c_g, s_p.astype(BF16)) * eac_x[:, sl]
        state_ref[hp] = s_p * jnp.concatenate(decs, axis=0) + _dot_tn(xdw[:, sl], b_g)
        ys.append(y_p + dskip_ref[:, sl] * xs[:, sl])
    y = jnp.concatenate(ys, axis=1) * _silu(zg)
    y = _rms(y) * gn_ref[...]
    y_ref[...] = y[:cs, :]

    @pl.when(c == pl.num_programs(1) - 1)
    def _():
        sn_ref[...] = state_ref[...]


def _ssd(xbc, z, dtr, s0, c0, wconv, bconv, dtb, alog, dskip, gn, batch):
    rows = xbc.shape[0]
    l = rows // batch
    cs = min(SSD_CHUNK, l)
    nc = l // cs
    pairs = SSD_WIDTH // LANES
    s0 = s0.reshape(batch, pairs, 2 * SSD_HEAD_DIM, SSD_STATE)
    i = lax.broadcasted_iota(jnp.int32, (SSD_CHUNK, SSD_CHUNK), 0)
    j = lax.broadcasted_iota(jnp.int32, (SSD_CHUNK, SSD_CHUNK), 1)
    tril = (j <= i).astype(BF16)
    eh = lax.broadcasted_iota(jnp.int32, (LANES, SSD_WIDTH), 0)
    ec = lax.broadcasted_iota(jnp.int32, (LANES, SSD_WIDTH), 1) // SSD_HEAD_DIM
    expand = (eh == ec).astype(BF16)
    row = lambda n: pl.BlockSpec((cs, n), lambda b, c: (b * nc + c, 0))
    vec = lambda a: pl.BlockSpec(a.shape, lambda b, c: (0, 0))
    st = pl.BlockSpec((None, pairs, 2 * SSD_HEAD_DIM, SSD_STATE), lambda b, c: (b, 0, 0, 0))
    cv = pl.BlockSpec((None, SSD_CONV - 1, CONV_DIM), lambda b, c: (b, 0, 0))
    y, sn, cn = pl.pallas_call(
        _ssd_kernel,
        grid=(batch, nc),
        in_specs=[row(CONV_DIM), row(SSD_WIDTH), row(LANES), st, cv,
                  vec(wconv), vec(bconv), vec(dtb), vec(alog), vec(dskip), vec(gn), vec(tril), vec(expand)],
        out_specs=(row(SSD_WIDTH), st, cv),
        out_shape=(jax.ShapeDtypeStruct((rows, SSD_WIDTH), F32),
                   jax.ShapeDtypeStruct(s0.shape, F32),
                   jax.ShapeDtypeStruct((batch, SSD_CONV - 1, CONV_DIM), F32)),
        scratch_shapes=[pltpu.VMEM((8 + cs, CONV_DIM), F32),
                        pltpu.VMEM((pairs, 2 * SSD_HEAD_DIM, SSD_STATE), F32)],
        compiler_params=_params("parallel", "arbitrary"),
        name="ssd",
    )(xbc, z, dtr, s0, c0, wconv, bconv, dtb, alog, dskip, gn, tril, expand)
    return y, sn.reshape(batch, SSD_HEADS, SSD_HEAD_DIM, SSD_STATE), cn


def _out_proj_kernel(o_ref, y_ref, x_ref, woa_ref, wob_ref, gpost_ref, gpre_ref, h_ref, hn_ref):
    mix = _dot(o_ref[...].astype(BF16), woa_ref[...]) + _dot(y_ref[...].astype(BF16), wob_ref[...])
    h = x_ref[...] + _rms(mix) * gpost_ref[...]
    h_ref[...] = h
    hn_ref[...] = (_rms(h) * gpre_ref[...]).astype(BF16)


def _out_proj(o, y, x, woa, wob, gpost, gpre):
    rows, d = x.shape
    tm = min(ROW_TILE, rows)
    row = lambda n: pl.BlockSpec((tm, n), lambda i: (i, 0))
    return pl.pallas_call(
        _out_proj_kernel,
        grid=(rows // tm,),
        in_specs=[row(ATT_WIDTH), row(SSD_WIDTH), row(d),
                  _const_spec(woa.shape), _const_spec(wob.shape), _const_spec(gpost.shape), _const_spec(gpre.shape)],
        out_specs=(row(d), row(d)),
        out_shape=(jax.ShapeDtypeStruct((rows, d), F32), jax.ShapeDtypeStruct((rows, d), BF16)),
        compiler_params=_params("parallel"),
        name="out_proj",
    )(o, y, x, woa, wob, gpost, gpre)


def _gelu_tanh(x):
    return 0.5 * x * (1.0 + jnp.tanh(0.7978845608028654 * (x + 0.044715 * (x * x * x))))


def _ffn_kernel(*refs, seq_len):
    if seq_len is None:
        hn_ref, halo_ref, wg_ref, wv_ref, cwg_ref, cwv_ref, cbg_ref, cbv_ref, wd_ref, f_ref, ug_ref, uv_ref, xe_ref = refs
    else:
        hn_ref, stg_ref, stv_ref, wg_ref, wv_ref, cwg_ref, cwv_ref, cbg_ref, cbv_ref, wd_ref, f_ref, ug_ref, uv_ref = refs
    i = pl.program_id(0)
    j = pl.program_id(1)
    tm = hn_ref.shape[0]

    if seq_len is None:
        @pl.when(j == 0)
        def _():
            halo = halo_ref[...]
            xe_ref[0:FFN_HALO, :] = jnp.where(i > 0, halo, jnp.zeros_like(halo))
            xe_ref[FFN_HALO:, :] = hn_ref[...]

        def conv(w_ref, cw_ref, cb_ref, u_ref):
            up = _dot(xe_ref[...], w_ref[...])
            cw = cw_ref[...]
            y = (cb_ref[...] + pltpu.roll(up, 2, 0)[FFN_HALO:] * cw[0:1]
                 + pltpu.roll(up, 1, 0)[FFN_HALO:] * cw[1:2] + up[FFN_HALO:] * cw[2:3])
            u_ref[...] = up[FFN_HALO + tm - 8:]
            return y
        gate = conv(wg_ref, cwg_ref, cbg_ref, ug_ref)
        val = conv(wv_ref, cwv_ref, cbv_ref, uv_ref)
    else:
        t = lax.broadcasted_iota(jnp.int32, (tm, 1), 0) % seq_len

        def conv(w_ref, cw_ref, cb_ref, st_ref, u_ref):
            up = _dot(hn_ref[...], w_ref[...])
            st = st_ref[...]
            cw = cw_ref[...]
            prev2 = jnp.where(t < 2, st, pltpu.roll(up, 2, 0))
            prev1 = jnp.where(t < 1, pltpu.roll(st, tm - 1, 0), pltpu.roll(up, 1, 0))
            u_ref[...] = up
            return cb_ref[...] + prev2 * cw[0:1] + prev1 * cw[1:2] + up * cw[2:3]
        gate = conv(wg_ref, cwg_ref, cbg_ref, stg_ref, ug_ref)
        val = conv(wv_ref, cwv_ref, cbv_ref, stv_ref, uv_ref)

    d = _dot((_gelu_tanh(gate) * val).astype(BF16), wd_ref[...])

    @pl.when(j == 0)
    def _():
        f_ref[...] = d

    @pl.when(j > 0)
    def _():
        f_ref[...] += d


def _ffn(hn, w_up, cw, cb, w_down, state=None, seq_len=None):
    rows, d = hn.shape
    dff = w_down.shape[0]
    tf = FFN_TF
    nj = dff // tf
    tm = min(ROW_TILE, rows)
    gcol = lambda r: pl.BlockSpec((r, tf), lambda i, j: (0, j))
    vcol = lambda r: pl.BlockSpec((r, tf), lambda i, j: (0, j + nj))
    hn_spec = pl.BlockSpec((tm, d), lambda i, j: (i, 0))
    common = [pl.BlockSpec((d, tf), lambda i, j: (0, j)), pl.BlockSpec((d, tf), lambda i, j: (0, j + nj)),
              gcol(FFN_CONV), vcol(FFN_CONV), gcol(1), vcol(1),
              pl.BlockSpec((tf, d), lambda i, j: (j, 0))]
    common_args = (w_up, w_up, cw, cw, cb, cb, w_down)
    f_spec = pl.BlockSpec((tm, d), lambda i, j: (i, 0))
    if seq_len is None:
        per = tm // FFN_HALO
        halo = pl.BlockSpec((FFN_HALO, d), lambda i, j: (jnp.maximum(i * per - 1, 0), 0))
        in_specs = [hn_spec, halo] + common
        args = (hn, hn) + common_args
        u_rows = 8 * (rows // tm)
        u_spec = pl.BlockSpec((8, tf), lambda i, j: (i, j))
        scratch = [pltpu.VMEM((FFN_HALO + tm, d), BF16)]
    else:
        assert rows == tm
        in_specs = [hn_spec, pl.BlockSpec((tm, tf), lambda i, j: (0, j)),
                    pl.BlockSpec((tm, tf), lambda i, j: (0, j + nj))] + common
        args = (hn, state, state) + common_args
        u_rows = tm
        u_spec = pl.BlockSpec((u_rows, tf), lambda i, j: (0, j))
        scratch = []
    return pl.pallas_call(
        functools.partial(_ffn_kernel, seq_len=seq_len),
        grid=(rows // tm, nj),
        in_specs=in_specs,
        out_specs=(f_spec, u_spec, u_spec),
        out_shape=(jax.ShapeDtypeStruct((rows, d), F32),
                   jax.ShapeDtypeStruct((u_rows, dff), F32),
                   jax.ShapeDtypeStruct((u_rows, dff), F32)),
        scratch_shapes=scratch,
        compiler_params=_params("arbitrary", "arbitrary"),
        name="ffn",
    )(*args)


def _ple_kernel(h_ref, f_ref, p_ref, gpost_ref, wg_ref, wp_ref, o_ref):
    h = h_ref[...] + _rms(f_ref[...]) * gpost_ref[...]
    gate = 1.0 / (1.0 + jnp.exp(-_dot(h.astype(BF16), wg_ref[...])))
    o_ref[...] = h + _dot(p_ref[...].astype(BF16), wp_ref[...]) * gate


def _ple(h, f, p, gpost, wg, wp):
    rows, d = h.shape
    tm = min(ROW_TILE, rows)
    row = lambda n: pl.BlockSpec((tm, n), lambda i: (i, 0))
    return pl.pallas_call(
        _ple_kernel,
        grid=(rows // tm,),
        in_specs=[row(d), row(d), row(p.shape[1]),
                  _const_spec(gpost.shape), _const_spec(wg.shape), _const_spec(wp.shape)],
        out_specs=row(d),
        out_shape=jax.ShapeDtypeStruct((rows, d), F32),
        compiler_params=_params("parallel"),
        name="ple",
    )(h, f, p, gpost, wg, wp)


def _layer(x, p, lw, batch, ssm0, conv0, ffn0, past):
    rows, d = x.shape
    l = rows // batch
    sample = past is not None
    q, kf, vf, kb, vb, z, xbc, dtr = _in_proj(x, lw["g_pre_mix"], lw["w_qkv"], lw["w_z"], lw["w_xbc"], lw["w_dt"],
                                                F32 if sample else BF16)
    if sample:
        page_table, cache_k, cache_v = past
        o = _attn_sample(page_table, lw["sb_bias"], q.reshape(batch, l, ATT_WIDTH), kf.reshape(batch, l, ATT_WIDTH),
                         vf.reshape(batch, l, ATT_WIDTH), cache_k, cache_v).reshape(rows, ATT_WIDTH)
    else:
        o = _attn_prompt(lw["sb_bias"], q, kb, vb)
    y, ssm_new, conv_new = _ssd(xbc, z, dtr, ssm0, conv0, lw["w_conv"], lw["b_conv"], lw["dt_bias"], lw["a_log"],
                                lw["d_skip"], lw["g_ssd_norm"], batch)
    h1, hn = _out_proj(o, y, x, lw["w_out_a"], lw["w_out_b"], lw["g_post_mix"], lw["g_pre_ffn"])
    dff2 = lw["w_up"].shape[1]
    if sample:
        st = jnp.pad(ffn0, ((0, 0), (0, l - (FFN_CONV - 1)), (0, 0))).reshape(rows, dff2)
        f, ug, uv = _ffn(hn, lw["w_up"], lw["w_ffn_conv"], lw["b_ffn_conv"], lw["w_down"], state=st, seq_len=l)
        up = jnp.concatenate([ug, uv], axis=1).reshape(batch, l, dff2)
        ffn_new = up[:, l - (FFN_CONV - 1):, :]
    else:
        f, ug, uv = _ffn(hn, lw["w_up"], lw["w_ffn_conv"], lw["b_ffn_conv"], lw["w_down"])
        ffn_new = jnp.concatenate([ug, uv], axis=1)[None, -(FFN_CONV - 1):, :]
    out = _ple(h1, f, p, lw["g_post_ffn"], lw["w_ple_gate"], lw["w_ple_proj"])
    return out, (kf, vf, ssm_new, conv_new, ffn_new)


def _prep_weights(i, g_pre_mix, w_in, sb_bias, w_conv, b_conv, dt_bias, a_log, d_skip, g_ssd_norm, w_out,
                  g_post_mix, g_pre_ffn, w_up, w_ffn_conv, b_ffn_conv, w_down, g_post_ffn, w_ple_gate, w_ple_proj):
    row = lambda a: a[i][None, :].astype(F32)
    lane_pad = lambda a: jnp.pad(a[i].astype(F32), (0, LANES - a.shape[1]))[None, :]
    w = w_in[i].astype(BF16)
    o_z = 3 * ATT_WIDTH
    o_x = o_z + SSD_WIDTH
    o_dt = o_x + CONV_DIM
    return {
        "g_pre_mix": row(g_pre_mix),
        "w_qkv": w[:, :o_z], "w_z": w[:, o_z:o_x], "w_xbc": w[:, o_x:o_dt],
        "w_dt": jnp.pad(w[:, o_dt:], ((0, 0), (0, LANES - SSD_HEADS))),
        "sb_bias": sb_bias[i].astype(F32),
        "w_conv": w_conv[i].astype(F32), "b_conv": row(b_conv),
        "dt_bias": lane_pad(dt_bias), "a_log": lane_pad(a_log),
        "d_skip": jnp.repeat(d_skip[i].astype(F32), SSD_HEAD_DIM)[None, :],
        "g_ssd_norm": row(g_ssd_norm),
        "w_out_a": w_out[i, :ATT_WIDTH].astype(BF16), "w_out_b": w_out[i, ATT_WIDTH:].astype(BF16),
        "g_post_mix": row(g_post_mix), "g_pre_ffn": row(g_pre_ffn),
        "w_up": w_up[i].astype(BF16), "w_ffn_conv": w_ffn_conv[i].astype(F32), "b_ffn_conv": row(b_ffn_conv),
        "w_down": w_down[i].astype(BF16), "g_post_ffn": row(g_post_ffn),
        "w_ple_gate": w_ple_gate[i].astype(BF16), "w_ple_proj": w_ple_proj[i].astype(BF16),
    }


def kernel(x_prompt, x_sample, cache_k, cache_v, state_ssm, state_conv, state_ffn_conv, page_table, p_prompt, p_sample, g_pre_mix, w_in, sb_bias, w_conv, b_conv, dt_bias, a_log, d_skip, g_ssd_norm, w_out, g_post_mix, g_pre_ffn, w_up, w_ffn_conv, b_ffn_conv, w_down, g_post_ffn, w_ple_gate, w_ple_proj):
    bp, lp, d = x_prompt.shape
    db, ls, _ = x_sample.shape
    depth = w_in.shape[0]
    dff2 = w_up.shape[2]
    hp = x_prompt.reshape(bp * lp, d)
    hs = x_sample.reshape(db * ls, d)
    outs_p, outs_s = [], []
    for i in range(depth):
        lw = _prep_weights(i, g_pre_mix, w_in, sb_bias, w_conv, b_conv, dt_bias, a_log, d_skip, g_ssd_norm, w_out,
                           g_post_mix, g_pre_ffn, w_up, w_ffn_conv, b_ffn_conv, w_down, g_post_ffn,
                           w_ple_gate, w_ple_proj)
        ssm0 = jnp.zeros((bp, SSD_HEADS, SSD_HEAD_DIM, SSD_STATE), F32)
        conv0 = jnp.zeros((bp, SSD_CONV - 1, CONV_DIM), F32)
        hp, st_p = _layer(hp, p_prompt[i].reshape(bp * lp, -1), lw, bp, ssm0, conv0, None, None)
        hs, st_s = _layer(hs, p_sample[i].reshape(db * ls, -1), lw, db, state_ssm[i], state_conv[i],
                          state_ffn_conv[i], (page_table, cache_k[i], cache_v[i]))
        outs_p.append(st_p)
        outs_s.append(st_s)

    def stack(outs, j, shape):
        return jnp.stack([o[j].reshape(shape) for o in outs], axis=0)

    kv_p = (bp, lp, ATT_HEADS, ATT_HEAD_DIM)
    kv_s = (db, ls, ATT_HEADS, ATT_HEAD_DIM)
    ssm_p = (bp, SSD_HEADS, SSD_HEAD_DIM, SSD_STATE)
    ssm_s = (db, SSD_HEADS, SSD_HEAD_DIM, SSD_STATE)
    return (hp.reshape(bp, lp, d), hs.reshape(db, ls, d),
            stack(outs_p, 0, kv_p), stack(outs_s, 0, kv_s), stack(outs_p, 1, kv_p), stack(outs_s, 1, kv_s),
            stack(outs_p, 2, ssm_p), stack(outs_s, 2, ssm_s),
            stack(outs_p, 3, (bp, SSD_CONV - 1, CONV_DIM)), stack(outs_s, 3, (db, SSD_CONV - 1, CONV_DIM)),
            stack(outs_p, 4, (bp, FFN_CONV - 1, dff2)), stack(outs_s, 4, (db, FFN_CONV - 1, dff2)))
```

```python
import functools

import jax
import jax.numpy as jnp
from jax import lax
from jax.experimental import pallas as pl
from jax.experimental.pallas import tpu as pltpu

F32 = jnp.float32
BF16 = jnp.bfloat16

ATT_HEADS = 16
ATT_HEAD_DIM = 64
ATT_WIDTH = ATT_HEADS * ATT_HEAD_DIM
SSD_HEADS = 16
SSD_HEAD_DIM = 64
SSD_WIDTH = SSD_HEADS * SSD_HEAD_DIM
SSD_GROUPS = 2
SSD_STATE = 128
SSD_CONV = 4
SSD_CHUNK = 128
CONV_DIM = SSD_WIDTH + 2 * SSD_GROUPS * SSD_STATE
FFN_CONV = 3
RMS_EPS = 1e-6
PAGE_SIZE = 128

LANES = 128
HEAD_PAIRS = ATT_WIDTH // LANES
VMEM_LIMIT = 56 * 1024 * 1024

ROW_TILE = 512
IN_PROJ_TILE = 256
ATT_TQ = 512
ATT_TK = 256
FFN_TM = 1024
FFN_TF = 512
FFN_HALO = 16
PAGES_PER_STEP = 8


def _params(*sem):
    return pltpu.CompilerParams(dimension_semantics=sem, vmem_limit_bytes=VMEM_LIMIT)


def _rms(x):
    return x * lax.rsqrt(jnp.mean(x * x, axis=-1, keepdims=True) + RMS_EPS)


def _softplus(x):
    return jnp.maximum(x, 0.0) + jnp.log(1.0 + jnp.exp(-jnp.abs(x)))


def _silu(x):
    return x * (1.0 / (1.0 + jnp.exp(-x)))


def _dot(a, b):
    return jnp.dot(a, b, preferred_element_type=F32)


def _dot_nt(a, b):
    return lax.dot_general(a, b, (((1,), (1,)), ((), ())), preferred_element_type=F32)


def _dot_tn(a, b):
    return lax.dot_general(a, b, (((0,), (0,)), ((), ())), preferred_element_type=F32)


def _split3(x):
    hi = x.astype(BF16)
    r1 = x - hi.astype(F32)
    mid = r1.astype(BF16)
    lo = (r1 - mid.astype(F32)).astype(BF16)
    return hi, mid, lo


def _dot_exact_rhs(m, x):
    hi, mid, lo = _split3(x)
    return _dot(m, hi) + _dot(m, mid) + _dot(m, lo)


def _dot_exact_lhs(x, m):
    hi, mid, lo = _split3(x)
    return _dot(hi, m) + _dot(mid, m) + _dot(lo, m)


def _const_spec(shape):
    return pl.BlockSpec(shape, lambda *_: (0,) * len(shape))


def _in_proj_kernel(x_ref, g_ref, wqkv_ref, wz_ref, wxbc_ref, wdt_ref,
                    q_ref, kf_ref, vf_ref, kb_ref, vb_ref, z_ref, xbc_ref, dt_ref):
    u = (_rms(x_ref[...]) * g_ref[...]).astype(BF16)
    qkv = _dot(u, wqkv_ref[...])
    q_ref[...] = (qkv[:, :ATT_WIDTH] * (ATT_HEAD_DIM ** -0.5)).astype(q_ref.dtype)
    k = qkv[:, ATT_WIDTH:2 * ATT_WIDTH]
    v = qkv[:, 2 * ATT_WIDTH:]
    kf_ref[...] = k
    vf_ref[...] = v
    kb_ref[...] = k.astype(BF16)
    vb_ref[...] = v.astype(BF16)
    z_ref[...] = _dot(u, wz_ref[...])
    xbc_ref[...] = _dot(u, wxbc_ref[...])
    dt_ref[...] = _dot(u, wdt_ref[...])


def _in_proj(x, g, wqkv, wz, wxbc, wdt, q_dtype):
    rows, d = x.shape
    tm = min(IN_PROJ_TILE, rows)
    row = lambda n: pl.BlockSpec((tm, n), lambda i: (i, 0))
    res = lambda a: pl.BlockSpec(a.shape, lambda i: (0, 0), pipeline_mode=pl.Buffered(1))
    out_shape = (
        jax.ShapeDtypeStruct((rows, ATT_WIDTH), q_dtype),
        jax.ShapeDtypeStruct((rows, ATT_WIDTH), F32),
        jax.ShapeDtypeStruct((rows, ATT_WIDTH), F32),
        jax.ShapeDtypeStruct((rows, ATT_WIDTH), BF16),
        jax.ShapeDtypeStruct((rows, ATT_WIDTH), BF16),
        jax.ShapeDtypeStruct((rows, SSD_WIDTH), F32),
        jax.ShapeDtypeStruct((rows, CONV_DIM), F32),
        jax.ShapeDtypeStruct((rows, LANES), F32),
    )
    return pl.pallas_call(
        _in_proj_kernel,
        grid=(rows // tm,),
        in_specs=[row(d), res(g), res(wqkv), res(wz), res(wxbc), res(wdt)],
        out_specs=tuple(row(s.shape[1]) for s in out_shape),
        out_shape=out_shape,
        compiler_params=_params("parallel"),
        name="in_proj",
    )(x, g, wqkv, wz, wxbc, wdt)


def _stick_block(z, carry, tri, mask):
    rows, tk = z.shape
    sp = _softplus(z)
    if mask is not None:
        sp = jnp.where(mask, sp, 0.0)
    cs = _dot(sp.astype(BF16), tri)
    wide = carry if tk == LANES else jnp.concatenate([carry] * (tk // LANES), axis=1)
    p = jnp.exp(z - cs[:, :tk] - wide)
    if mask is not None:
        p = jnp.where(mask, p, 0.0)
    total = cs[:, tk:] if tri.shape[1] > tk else jnp.broadcast_to(cs[:, 0:1], (rows, LANES))
    return p, carry + total


def _tri(tk, with_ones):
    n = tk + LANES if with_ones else tk
    s = lax.broadcasted_iota(jnp.int32, (tk, n), 0)
    j = lax.broadcasted_iota(jnp.int32, (tk, n), 1)
    return ((s >= j) | (j >= tk)).astype(BF16)


def _attn_prompt_kernel(q_ref, qfill_ref, kfill_ref, k_ref, v_ref, tri_ref, o_ref, acc_ref, c_ref, z_ref, p_ref):
    qi = pl.program_id(1)
    tq, tk = ATT_TQ, ATT_TK
    n_blocks = (qi + 1) * (tq // tk)
    low = lax.broadcasted_iota(jnp.int32, (1, LANES), 1) < ATT_HEAD_DIM
    own = (low, jnp.logical_not(low))
    q = q_ref[...]
    qs = tuple(jnp.where(own[a], q, qfill_ref[a]) for a in range(2))
    tri = tri_ref[...]
    acc_ref[...] = jnp.zeros_like(acc_ref)
    c_ref[...] = jnp.zeros_like(c_ref)
    q_pos = qi * tq + lax.broadcasted_iota(jnp.int32, (tq, tk), 0)
    k_off = lax.broadcasted_iota(jnp.int32, (tq, tk), 1)

    def rows(n):
        return pl.ds(pl.multiple_of(jnp.maximum(n_blocks - 1 - n, 0) * tk, tk), tk)

    def scores(n, slot):
        kb = k_ref[rows(n), :]
        for a in range(2):
            z_ref[slot, a] = _dot_nt(qs[a], jnp.where(own[a], kb, kfill_ref[a]))

    def weights(n, slot, masked):
        mask = ((n_blocks - 1 - n) * tk + k_off) < q_pos if masked else None
        for a in range(2):
            p, c_ref[a] = _stick_block(z_ref[slot, a], c_ref[a], tri, mask)
            p_ref[slot, :, a * tk:(a + 1) * tk] = p.astype(BF16)

    def apply(n, slot):
        vb = v_ref[rows(n), :]
        vz = jnp.zeros_like(vb)
        vcat = jnp.concatenate([jnp.where(low, vb, vz), jnp.where(low, vz, vb)], axis=0)
        acc_ref[...] += _dot(p_ref[slot], vcat)

    def pair(u, masked, first):
        n = 2 * u
        scores(n + 1, 1)
        if not first:
            apply(n - 1, 1)
        weights(n, 0, masked)
        scores(n + 2, 0)
        apply(n, 0)
        weights(n + 1, 1, masked)

    def body(u, carry):
        pair(u, False, False)
        return carry

    scores(0, 0)
    pair(0, True, True)
    lax.fori_loop(1, n_blocks // 2, body, 0)
    apply(n_blocks - 1, 1)
    o_ref[...] = acc_ref[...].astype(o_ref.dtype)


def _bias_lanes(bias):
    b = bias.astype(F32)
    b0 = b.astype(BF16)
    r = b - b0.astype(F32)
    b1 = r.astype(BF16)
    b2 = (r - b1.astype(F32)).astype(BF16)
    terms = jnp.stack([b0, b1, b2], axis=-1).reshape(HEAD_PAIRS, 2, 3)
    pad = jnp.zeros((HEAD_PAIRS, ATT_HEAD_DIM - 3), BF16)
    zero = jnp.zeros((HEAD_PAIRS, ATT_HEAD_DIM), BF16)
    first = jnp.concatenate([zero, terms[:, 0], pad], axis=1)
    second = jnp.concatenate([terms[:, 1], pad, zero], axis=1)
    qfill = jnp.stack([first, second], axis=1)
    ones = jnp.concatenate([jnp.ones((3,), BF16), jnp.zeros((ATT_HEAD_DIM - 3,), BF16)])
    zeros = jnp.zeros((ATT_HEAD_DIM,), BF16)
    kfill = jnp.stack([jnp.concatenate([zeros, ones]), jnp.concatenate([ones, zeros])])
    return qfill, kfill


def _attn_prompt(bias, q, k, v):
    l = q.shape[0]
    tq, tk = ATT_TQ, ATT_TK
    assert tq == 2 * tk and l % tq == 0
    qfill, kfill = _bias_lanes(bias)
    qfill = jnp.broadcast_to(qfill[:, :, None, :], (HEAD_PAIRS, 2, tq, LANES))
    kfill = jnp.broadcast_to(kfill[:, None, :], (2, tk, LANES))
    return pl.pallas_call(
        _attn_prompt_kernel,
        grid=(HEAD_PAIRS, l // tq),
        in_specs=[
            pl.BlockSpec((tq, LANES), lambda hp, qi: (qi, hp)),
            pl.BlockSpec((None, 2, tq, LANES), lambda hp, qi: (hp, 0, 0, 0)),
            pl.BlockSpec((2, tk, LANES), lambda hp, qi: (0, 0, 0)),
            pl.BlockSpec((l, LANES), lambda hp, qi: (0, hp)),
            pl.BlockSpec((l, LANES), lambda hp, qi: (0, hp)),
            pl.BlockSpec((tk, tk), lambda hp, qi: (0, 0)),
        ],
        out_specs=pl.BlockSpec((tq, LANES), lambda hp, qi: (qi, hp)),
        out_shape=jax.ShapeDtypeStruct((l, ATT_WIDTH), BF16),
        scratch_shapes=[pltpu.VMEM((tq, LANES), F32), pltpu.VMEM((2, tq, LANES), F32),
                        pltpu.VMEM((2, 2, tq, tk), F32), pltpu.VMEM((2, tq, 2 * tk), BF16)],
        compiler_params=_params("parallel", "parallel"),
        name="attn_prompt",
    )(q, qfill, kfill, k, v, _tri(tk, False))


def _attn_sample_kernel(pt_ref, q_ref, kn_ref, vn_ref, brow_ref, tri_ref, *rest):
    n = PAGES_PER_STEP
    k_refs, v_refs = rest[:n], rest[n:2 * n]
    o_ref, qexp_ref, acc_ref, c_ref = rest[2 * n:]
    s = pl.program_id(1)
    rows = ATT_HEADS * q_ref.shape[0]
    lq = q_ref.shape[0]
    col_head = lax.broadcasted_iota(jnp.int32, (1, ATT_WIDTH), 1) // ATT_HEAD_DIM
    tri = tri_ref[...]
    brow = brow_ref[...]

    @pl.when(s == 0)
    def _():
        qt = jnp.concatenate([q_ref[...]] * ATT_HEADS, axis=0)
        row_head = lax.broadcasted_iota(jnp.int32, (rows, 1), 0) // lq
        qexp_ref[...] = jnp.where(row_head == col_head, qt, 0.0).astype(BF16)
        c_ref[...] = jnp.zeros_like(c_ref)
        pad = jnp.zeros((PAGE_SIZE - lq, ATT_WIDTH), F32)
        kn = jnp.concatenate([kn_ref[...], pad], axis=0)
        vn = jnp.concatenate([vn_ref[...], pad], axis=0)
        t = lax.broadcasted_iota(jnp.int32, (rows, PAGE_SIZE), 0) % lq
        tok = lax.broadcasted_iota(jnp.int32, (rows, PAGE_SIZE), 1)
        z = _dot_nt(qexp_ref[...], kn.astype(BF16)) + brow
        p, c_ref[...] = _stick_block(z, jnp.zeros_like(brow), tri, tok < t)
        acc_ref[...] = _dot(p.astype(BF16), vn.astype(BF16))

    kcat = jnp.concatenate([k_refs[i][...].astype(BF16) for i in reversed(range(n))], axis=1)
    vcat = jnp.concatenate([v_refs[i][...].astype(BF16) for i in reversed(range(n))], axis=1)
    z = _dot(qexp_ref[...], kcat)
    c = c_ref[...]
    ps = [None] * n
    for i in reversed(range(n)):
        zi = z[:, i * PAGE_SIZE:(i + 1) * PAGE_SIZE] + brow
        p, c = _stick_block(zi, c, tri, None)
        ps[i] = p.astype(BF16)
    c_ref[...] = c
    acc_ref[...] += _dot_nt(jnp.concatenate(ps, axis=1), vcat)

    @pl.when(s == pl.num_programs(1) - 1)
    def _():
        out = jnp.zeros((lq, ATT_WIDTH), F32)
        for h in range(ATT_HEADS):
            out = jnp.where(col_head == h, acc_ref[h * lq:(h + 1) * lq, :], out)
        o_ref[...] = out


def _attn_sample(page_table, bias, q, k_new, v_new, cache_k, cache_v):
    db, lq, _ = q.shape
    n_pages = page_table.shape[1]
    n = PAGES_PER_STEP
    rows = ATT_HEADS * lq
    pool = cache_k.shape[0]
    ck = jnp.transpose(cache_k, (0, 2, 3, 1)).reshape(pool, ATT_WIDTH, PAGE_SIZE)
    cv = jnp.transpose(cache_v, (0, 2, 3, 1)).reshape(pool, ATT_WIDTH, PAGE_SIZE)
    brow = jnp.broadcast_to(jnp.repeat(bias.astype(F32), lq)[:, None], (rows, LANES))
    seq = pl.BlockSpec((None, lq, ATT_WIDTH), lambda b, s, pt: (b, 0, 0))

    def page(i):
        return pl.BlockSpec((None, ATT_WIDTH, PAGE_SIZE),
                            lambda b, s, pt: (pt[b, n_pages - 1 - (s * n + i)], 0, 0))

    grid_spec = pltpu.PrefetchScalarGridSpec(
        num_scalar_prefetch=1,
        grid=(db, n_pages // n),
        in_specs=[seq, seq, seq,
                  pl.BlockSpec((rows, LANES), lambda b, s, pt: (0, 0)),
                  pl.BlockSpec((PAGE_SIZE, PAGE_SIZE + LANES), lambda b, s, pt: (0, 0))]
                 + [page(i) for i in range(n)] * 2,
        out_specs=seq,
        scratch_shapes=[pltpu.VMEM((rows, ATT_WIDTH), BF16),
                        pltpu.VMEM((rows, ATT_WIDTH), F32),
                        pltpu.VMEM((rows, LANES), F32)],
    )
    return pl.pallas_call(
        _attn_sample_kernel,
        grid_spec=grid_spec,
        out_shape=jax.ShapeDtypeStruct((db, lq, ATT_WIDTH), F32),
        compiler_params=_params("parallel", "arbitrary"),
        name="attn_sample",
    )(page_table, q, k_new, v_new, brow, _tri(PAGE_SIZE, True), *([ck] * n), *([cv] * n))


def _ssd_kernel(xbc_ref, z_ref, dtr_ref, s0_ref, c0_ref, wconv_ref, bconv_ref, dtb_ref, alog_ref,
                dskip_ref, gn_ref, tril_ref, e_ref,
                y_ref, sn_ref, cn_ref, cbuf_ref, state_ref):
    c = pl.program_id(1)
    cs = xbc_ref.shape[0]
    ck = SSD_CHUNK
    tail = 8
    hist = SSD_CONV - 1

    @pl.when(c == 0)
    def _():
        state_ref[...] = s0_ref[...]
        cbuf_ref[0:tail, :] = jnp.zeros((tail, CONV_DIM), F32)
        cbuf_ref[tail - hist:tail, :] = c0_ref[...]

    cbuf_ref[tail:tail + cs, :] = xbc_ref[...]
    w = wconv_ref[...]
    xc = bconv_ref[...] + cbuf_ref[tail - hist:tail - hist + cs, :] * w[0:1]
    for j in range(1, SSD_CONV):
        xc = xc + cbuf_ref[tail - hist + j:tail - hist + j + cs, :] * w[j:j + 1]
    cn_ref[...] = cbuf_ref[tail + cs - hist:tail + cs, :]
    last = cbuf_ref[cs:cs + tail, :]
    cbuf_ref[0:tail, :] = last
    xc = _silu(xc)
    dt = _softplus(dtr_ref[...] + dtb_ref[...])
    zg = z_ref[...]
    if cs < ck:
        xc = jnp.concatenate([xc, jnp.zeros((ck - cs, CONV_DIM), F32)], axis=0)
        dt = jnp.concatenate([dt, jnp.zeros((ck - cs, LANES), F32)], axis=0)
        zg = jnp.concatenate([zg, jnp.zeros((ck - cs, SSD_WIDTH), F32)], axis=0)

    a = -jnp.exp(alog_ref[...])
    acum = _dot_exact_rhs(tril_ref[...], dt * a)
    acum_row = acum.T
    e = e_ref[...]
    dt_x = _dot_exact_lhs(dt, e)
    ac_x = _dot_exact_lhs(acum, e)
    xs = xc[:, :SSD_WIDTH]
    xd = xs * dt_x
    xdw = (xd * jnp.exp(ac_x[ck - 1:ck, :] - ac_x)).astype(BF16)
    eac_x = jnp.exp(ac_x)
    row_i = lax.broadcasted_iota(jnp.int32, (ck, ck), 0)
    col_j = lax.broadcasted_iota(jnp.int32, (ck, ck), 1)
    causal = row_i >= col_j
    low = lax.broadcasted_iota(jnp.int32, (1, LANES), 1) < SSD_HEAD_DIM
    per_group = SSD_HEADS // SSD_GROUPS
    ys = []
    gmat = {}
    for hp in range(SSD_WIDTH // LANES):
        g = (2 * hp) // per_group
        b_g = xc[:, SSD_WIDTH + g * SSD_STATE:SSD_WIDTH + (g + 1) * SSD_STATE].astype(BF16)
        c_off = SSD_WIDTH + SSD_GROUPS * SSD_STATE
        c_g = xc[:, c_off + g * SSD_STATE:c_off + (g + 1) * SSD_STATE].astype(BF16)
        if g not in gmat:
            gmat[g] = _dot_nt(c_g, b_g)
        sl = slice(hp * LANES, (hp + 1) * LANES)
        xd_p = xd[:, sl]
        y_p = None
        decs = []
        for k in range(2):
            h = 2 * hp + k
            diff = acum[:, h:h + 1] - acum_row[h:h + 1, :]
            decay = jnp.where(causal, jnp.exp(jnp.minimum(diff, 0.0)), 0.0)
            sc = (gmat[g] * decay).astype(BF16)
            xd_k = jnp.where(low if k == 0 else jnp.logical_not(low), xd_p, 0.0).astype(BF16)
            d = _dot(sc, xd_k)
            y_p = d if y_p is None else y_p + d
            decs.append(jnp.broadcast_to(jnp.exp(acum[ck - 1:ck, h:h + 1]), (SSD_HEAD_DIM, SSD_STATE)))
        s_p = state_ref[hp]
        y_p = y_p + _dot_nt(c_g, s_p.astype(BF16)) * eac_x[:, sl]
        state_ref[hp] = s_p * jnp.concatenate(decs, axis=0) + _dot_tn(xdw[:, sl], b_g)
        ys.append(y_p + dskip_ref[:, sl] * xs[:, sl])
    y = jnp.concatenate(ys, axis=1) * _silu(zg)
    y = _rms(y) * gn_ref[...]
    y_ref[...] = y[:cs, :]

    @pl.when(c == pl.num_programs(1) - 1)
    def _():
        sn_ref[...] = state_ref[...]


def _ssd(xbc, z, dtr, s0, c0, wconv, bconv, dtb, alog, dskip, gn, batch):
    rows = xbc.shape[0]
    l = rows // batch
    cs = min(SSD_CHUNK, l)
    nc = l // cs
    pairs = SSD_WIDTH // LANES
    s0 = s0.reshape(batch, pairs, 2 * SSD_HEAD_DIM, SSD_STATE)
    i = lax.broadcasted_iota(jnp.int32, (SSD_CHUNK, SSD_CHUNK), 0)
    j = lax.broadcasted_iota(jnp.int32, (SSD_CHUNK, SSD_CHUNK), 1)
    tril = (j <= i).astype(BF16)
    eh = lax.broadcasted_iota(jnp.int32, (LANES, SSD_WIDTH), 0)
    ec = lax.broadcasted_iota(jnp.int32, (LANES, SSD_WIDTH), 1) // SSD_HEAD_DIM
    expand = (eh == ec).astype(BF16)
    row = lambda n: pl.BlockSpec((cs, n), lambda b, c: (b * nc + c, 0))
    vec = lambda a: pl.BlockSpec(a.shape, lambda b, c: (0, 0))
    st = pl.BlockSpec((None, pairs, 2 * SSD_HEAD_DIM, SSD_STATE), lambda b, c: (b, 0, 0, 0))
    cv = pl.BlockSpec((None, SSD_CONV - 1, CONV_DIM), lambda b, c: (b, 0, 0))
    y, sn, cn = pl.pallas_call(
        _ssd_kernel,
        grid=(batch, nc),
        in_specs=[row(CONV_DIM), row(SSD_WIDTH), row(LANES), st, cv,
                  vec(wconv), vec(bconv), vec(dtb), vec(alog), vec(dskip), vec(gn), vec(tril), vec(expand)],
        out_specs=(row(SSD_WIDTH), st, cv),
        out_shape=(jax.ShapeDtypeStruct((rows, SSD_WIDTH), F32),
                   jax.ShapeDtypeStruct(s0.shape, F32),
                   jax.ShapeDtypeStruct((batch, SSD_CONV - 1, CONV_DIM), F32)),
        scratch_shapes=[pltpu.VMEM((8 + cs, CONV_DIM), F32),
                        pltpu.VMEM((pairs, 2 * SSD_HEAD_DIM, SSD_STATE), F32)],
        compiler_params=_params("parallel", "arbitrary"),
        name="ssd",
    )(xbc, z, dtr, s0, c0, wconv, bconv, dtb, alog, dskip, gn, tril, expand)
    return y, sn.reshape(batch, SSD_HEADS, SSD_HEAD_DIM, SSD_STATE), cn


def _out_proj_kernel(o_ref, y_ref, x_ref, woa_ref, wob_ref, gpost_ref, gpre_ref, h_ref, hn_ref):
    mix = _dot(o_ref[...].astype(BF16), woa_ref[...]) + _dot(y_ref[...].astype(BF16), wob_ref[...])
    h = x_ref[...] + _rms(mix) * gpost_ref[...]
    h_ref[...] = h
    hn_ref[...] = (_rms(h) * gpre_ref[...]).astype(BF16)


def _out_proj(o, y, x, woa, wob, gpost, gpre):
    rows, d = x.shape
    tm = min(ROW_TILE, rows)
    row = lambda n: pl.BlockSpec((tm, n), lambda i: (i, 0))
    return pl.pallas_call(
        _out_proj_kernel,
        grid=(rows // tm,),
        in_specs=[row(ATT_WIDTH), row(SSD_WIDTH), row(d),
                  _const_spec(woa.shape), _const_spec(wob.shape), _const_spec(gpost.shape), _const_spec(gpre.shape)],
        out_specs=(row(d), row(d)),
        out_shape=(jax.ShapeDtypeStruct((rows, d), F32), jax.ShapeDtypeStruct((rows, d), BF16)),
        compiler_params=_params("parallel"),
        name="out_proj",
    )(o, y, x, woa, wob, gpost, gpre)


def _gelu_tanh(x):
    return 0.5 * x * (1.0 + jnp.tanh(0.7978845608028654 * (x + 0.044715 * (x * x * x))))


def _ffn_kernel(*refs, seq_len):
    if seq_len is None:
        hn_ref, halo_ref, wg_ref, wv_ref, cwg_ref, cwv_ref, cbg_ref, cbv_ref, wd_ref, f_ref, ug_ref, uv_ref, xe_ref = refs
    else:
        hn_ref, stg_ref, stv_ref, wg_ref, wv_ref, cwg_ref, cwv_ref, cbg_ref, cbv_ref, wd_ref, f_ref, ug_ref, uv_ref = refs
    i = pl.program_id(0)
    j = pl.program_id(1)
    tm = hn_ref.shape[0]

    @pl.when(j == 0)
    def _():
        f_ref[...] = jnp.zeros_like(f_ref)

    if seq_len is None:
        @pl.when(j == 0)
        def _():
            halo = halo_ref[...]
            xe_ref[0:FFN_HALO, :] = jnp.where(i > 0, halo, jnp.zeros_like(halo))
            xe_ref[FFN_HALO:, :] = hn_ref[...]

        def conv(w_ref, cw_ref, cb_ref, u_ref):
            up = _dot(xe_ref[...], w_ref[...])
            cw = cw_ref[...]
            y = (cb_ref[...] + pltpu.roll(up, 2, 0)[FFN_HALO:] * cw[0:1]
                 + pltpu.roll(up, 1, 0)[FFN_HALO:] * cw[1:2] + up[FFN_HALO:] * cw[2:3])
            u_ref[...] = up[FFN_HALO + tm - 8:]
            return y
        gate = conv(wg_ref, cwg_ref, cbg_ref, ug_ref)
        val = conv(wv_ref, cwv_ref, cbv_ref, uv_ref)
    else:
        t = lax.broadcasted_iota(jnp.int32, (tm, 1), 0) % seq_len

        def conv(w_ref, cw_ref, cb_ref, st_ref, u_ref):
            up = _dot(hn_ref[...], w_ref[...])
            st = st_ref[...]
            cw = cw_ref[...]
            prev2 = jnp.where(t < 2, st, pltpu.roll(up, 2, 0))
            prev1 = jnp.where(t < 1, pltpu.roll(st, tm - 1, 0), pltpu.roll(up, 1, 0))
            u_ref[...] = up
            return cb_ref[...] + prev2 * cw[0:1] + prev1 * cw[1:2] + up * cw[2:3]
        gate = conv(wg_ref, cwg_ref, cbg_ref, stg_ref, ug_ref)
        val = conv(wv_ref, cwv_ref, cbv_ref, stv_ref, uv_ref)

    f_ref[...] += _dot((_gelu_tanh(gate) * val).astype(BF16), wd_ref[...])


def _ffn(hn, w_up, cw, cb, w_down, state=None, seq_len=None):
    rows, d = hn.shape
    dff = w_down.shape[0]
    tf = FFN_TF
    nj = dff // tf
    tm = min(FFN_TM, rows)
    gcol = lambda r: pl.BlockSpec((r, tf), lambda i, j: (0, j))
    vcol = lambda r: pl.BlockSpec((r, tf), lambda i, j: (0, j + nj))
    hn_spec = pl.BlockSpec((tm, d), lambda i, j: (i, 0))
    common = [pl.BlockSpec((d, tf), lambda i, j: (0, j)), pl.BlockSpec((d, tf), lambda i, j: (0, j + nj)),
              gcol(FFN_CONV), vcol(FFN_CONV), gcol(1), vcol(1),
              pl.BlockSpec((tf, d), lambda i, j: (j, 0))]
    common_args = (w_up, w_up, cw, cw, cb, cb, w_down)
    f_spec = pl.BlockSpec((tm, d), lambda i, j: (i, 0))
    if seq_len is None:
        per = tm // FFN_HALO
        halo = pl.BlockSpec((FFN_HALO, d), lambda i, j: (jnp.maximum(i * per - 1, 0), 0))
        in_specs = [hn_spec, halo] + common
        args = (hn, hn) + common_args
        u_rows = 8 * (rows // tm)
        u_spec = pl.BlockSpec((8, tf), lambda i, j: (i, j))
        scratch = [pltpu.VMEM((FFN_HALO + tm, d), BF16)]
    else:
        assert rows == tm
        in_specs = [hn_spec, pl.BlockSpec((tm, tf), lambda i, j: (0, j)),
                    pl.BlockSpec((tm, tf), lambda i, j: (0, j + nj))] + common
        args = (hn, state, state) + common_args
        u_rows = tm
        u_spec = pl.BlockSpec((u_rows, tf), lambda i, j: (0, j))
        scratch = []
    return pl.pallas_call(
        functools.partial(_ffn_kernel, seq_len=seq_len),
        grid=(rows // tm, nj),
        in_specs=in_specs,
        out_specs=(f_spec, u_spec, u_spec),
        out_shape=(jax.ShapeDtypeStruct((rows, d), F32),
                   jax.ShapeDtypeStruct((u_rows, dff), F32),
                   jax.ShapeDtypeStruct((u_rows, dff), F32)),
        scratch_shapes=scratch,
        compiler_params=_params("arbitrary", "arbitrary"),
        name="ffn",
    )(*args)


def _ple_kernel(h_ref, f_ref, p_ref, gpost_ref, wg_ref, wp_ref, o_ref):
    h = h_ref[...] + _rms(f_ref[...]) * gpost_ref[...]
    gate = 1.0 / (1.0 + jnp.exp(-_dot(h.astype(BF16), wg_ref[...])))
    o_ref[...] = h + _dot(p_ref[...].astype(BF16), wp_ref[...]) * gate


def _ple(h, f, p, gpost, wg, wp):
    rows, d = h.shape
    tm = min(ROW_TILE, rows)
    row = lambda n: pl.BlockSpec((tm, n), lambda i: (i, 0))
    return pl.pallas_call(
        _ple_kernel,
        grid=(rows // tm,),
        in_specs=[row(d), row(d), row(p.shape[1]),
                  _const_spec(gpost.shape), _const_spec(wg.shape), _const_spec(wp.shape)],
        out_specs=row(d),
        out_shape=jax.ShapeDtypeStruct((rows, d), F32),
        compiler_params=_params("parallel"),
        name="ple",
    )(h, f, p, gpost, wg, wp)


def _layer(x, p, lw, batch, ssm0, conv0, ffn0, past):
    rows, d = x.shape
    l = rows // batch
    sample = past is not None
    q, kf, vf, kb, vb, z, xbc, dtr = _in_proj(x, lw["g_pre_mix"], lw["w_qkv"], lw["w_z"], lw["w_xbc"], lw["w_dt"],
                                                F32 if sample else BF16)
    if sample:
        page_table, cache_k, cache_v = past
        o = _attn_sample(page_table, lw["sb_bias"], q.reshape(batch, l, ATT_WIDTH), kf.reshape(batch, l, ATT_WIDTH),
                         vf.reshape(batch, l, ATT_WIDTH), cache_k, cache_v).reshape(rows, ATT_WIDTH)
    else:
        o = _attn_prompt(lw["sb_bias"], q, kb, vb)
    y, ssm_new, conv_new = _ssd(xbc, z, dtr, ssm0, conv0, lw["w_conv"], lw["b_conv"], lw["dt_bias"], lw["a_log"],
                                lw["d_skip"], lw["g_ssd_norm"], batch)
    h1, hn = _out_proj(o, y, x, lw["w_out_a"], lw["w_out_b"], lw["g_post_mix"], lw["g_pre_ffn"])
    dff2 = lw["w_up"].shape[1]
    if sample:
        st = jnp.pad(ffn0, ((0, 0), (0, l - (FFN_CONV - 1)), (0, 0))).reshape(rows, dff2)
        f, ug, uv = _ffn(hn, lw["w_up"], lw["w_ffn_conv"], lw["b_ffn_conv"], lw["w_down"], state=st, seq_len=l)
        up = jnp.concatenate([ug, uv], axis=1).reshape(batch, l, dff2)
        ffn_new = up[:, l - (FFN_CONV - 1):, :]
    else:
        f, ug, uv = _ffn(hn, lw["w_up"], lw["w_ffn_conv"], lw["b_ffn_conv"], lw["w_down"])
        ffn_new = jnp.concatenate([ug, uv], axis=1)[None, -(FFN_CONV - 1):, :]
    out = _ple(h1, f, p, lw["g_post_ffn"], lw["w_ple_gate"], lw["w_ple_proj"])
    return out, (kf, vf, ssm_new, conv_new, ffn_new)


def _prep_weights(i, g_pre_mix, w_in, sb_bias, w_conv, b_conv, dt_bias, a_log, d_skip, g_ssd_norm, w_out,
                  g_post_mix, g_pre_ffn, w_up, w_ffn_conv, b_ffn_conv, w_down, g_post_ffn, w_ple_gate, w_ple_proj):
    row = lambda a: a[i][None, :].astype(F32)
    lane_pad = lambda a: jnp.pad(a[i].astype(F32), (0, LANES - a.shape[1]))[None, :]
    w = w_in[i].astype(BF16)
    o_z = 3 * ATT_WIDTH
    o_x = o_z + SSD_WIDTH
    o_dt = o_x + CONV_DIM
    return {
        "g_pre_mix": row(g_pre_mix),
        "w_qkv": w[:, :o_z], "w_z": w[:, o_z:o_x], "w_xbc": w[:, o_x:o_dt],
        "w_dt": jnp.pad(w[:, o_dt:], ((0, 0), (0, LANES - SSD_HEADS))),
        "sb_bias": sb_bias[i].astype(F32),
        "w_conv": w_conv[i].astype(F32), "b_conv": row(b_conv),
        "dt_bias": lane_pad(dt_bias), "a_log": lane_pad(a_log),
        "d_skip": jnp.repeat(d_skip[i].astype(F32), SSD_HEAD_DIM)[None, :],
        "g_ssd_norm": row(g_ssd_norm),
        "w_out_a": w_out[i, :ATT_WIDTH].astype(BF16), "w_out_b": w_out[i, ATT_WIDTH:].astype(BF16),
        "g_post_mix": row(g_post_mix), "g_pre_ffn": row(g_pre_ffn),
        "w_up": w_up[i].astype(BF16), "w_ffn_conv": w_ffn_conv[i].astype(F32), "b_ffn_conv": row(b_ffn_conv),
        "w_down": w_down[i].astype(BF16), "g_post_ffn": row(g_post_ffn),
        "w_ple_gate": w_ple_gate[i].astype(BF16), "w_ple_proj": w_ple_proj[i].astype(BF16),
    }


def kernel(x_prompt, x_sample, cache_k, cache_v, state_ssm, state_conv, state_ffn_conv, page_table, p_prompt, p_sample, g_pre_mix, w_in, sb_bias, w_conv, b_conv, dt_bias, a_log, d_skip, g_ssd_norm, w_out, g_post_mix, g_pre_ffn, w_up, w_ffn_conv, b_ffn_conv, w_down, g_post_ffn, w_ple_gate, w_ple_proj):
    bp, lp, d = x_prompt.shape
    db, ls, _ = x_sample.shape
    depth = w_in.shape[0]
    dff2 = w_up.shape[2]
    hp = x_prompt.reshape(bp * lp, d)
    hs = x_sample.reshape(db * ls, d)
    outs_p, outs_s = [], []
    for i in range(depth):
        lw = _prep_weights(i, g_pre_mix, w_in, sb_bias, w_conv, b_conv, dt_bias, a_log, d_skip, g_ssd_norm, w_out,
                           g_post_mix, g_pre_ffn, w_up, w_ffn_conv, b_ffn_conv, w_down, g_post_ffn,
                           w_ple_gate, w_ple_proj)
        ssm0 = jnp.zeros((bp, SSD_HEADS, SSD_HEAD_DIM, SSD_STATE), F32)
        conv0 = jnp.zeros((bp, SSD_CONV - 1, CONV_DIM), F32)
        hp, st_p = _layer(hp, p_prompt[i].reshape(bp * lp, -1), lw, bp, ssm0, conv0, None, None)
        hs, st_s = _layer(hs, p_sample[i].reshape(db * ls, -1), lw, db, state_ssm[i], state_conv[i],
                          state_ffn_conv[i], (page_table, cache_k[i], cache_v[i]))
        outs_p.append(st_p)
        outs_s.append(st_s)

    def stack(outs, j, shape):
        return jnp.stack([o[j].reshape(shape) for o in outs], axis=0)

    kv_p = (bp, lp, ATT_HEADS, ATT_HEAD_DIM)
    kv_s = (db, ls, ATT_HEADS, ATT_HEAD_DIM)
    ssm_p = (bp, SSD_HEADS, SSD_HEAD_DIM, SSD_STATE)
    ssm_s = (db, SSD_HEADS, SSD_HEAD_DIM, SSD_STATE)
    return (hp.reshape(bp, lp, d), hs.reshape(db, ls, d),
            stack(outs_p, 0, kv_p), stack(outs_s, 0, kv_s), stack(outs_p, 1, kv_p), stack(outs_s, 1, kv_s),
            stack(outs_p, 2, ssm_p), stack(outs_s, 2, ssm_s),
            stack(outs_p, 3, (bp, SSD_CONV - 1, CONV_DIM)), stack(outs_s, 3, (db, SSD_CONV - 1, CONV_DIM)),
            stack(outs_p, 4, (bp, FFN_CONV - 1, dff2)), stack(outs_s, 4, (db, FFN_CONV - 1, dff2)))
```

```python
import functools

import jax
import jax.numpy as jnp
from jax import lax
from jax.experimental import pallas as pl
from jax.experimental.pallas import tpu as pltpu

F32 = jnp.float32
BF16 = jnp.bfloat16

ATT_HEADS = 16
ATT_HEAD_DIM = 64
ATT_WIDTH = ATT_HEADS * ATT_HEAD_DIM
SSD_HEADS = 16
SSD_HEAD_DIM = 64
SSD_WIDTH = SSD_HEADS * SSD_HEAD_DIM
SSD_GROUPS = 2
SSD_STATE = 128
SSD_CONV = 4
SSD_CHUNK = 128
CONV_DIM = SSD_WIDTH + 2 * SSD_GROUPS * SSD_STATE
FFN_CONV = 3
RMS_EPS = 1e-6
PAGE_SIZE = 128

LANES = 128
HEAD_PAIRS = ATT_WIDTH // LANES
VMEM_LIMIT = 56 * 1024 * 1024

ROW_TILE = 512
IN_PROJ_TILE = 256
ATT_TQ = 512
ATT_TK = 256
FFN_TM = 1024
FFN_TF = 512
FFN_HALO = 16
PAGES_PER_STEP = 4


def _params(*sem):
    return pltpu.CompilerParams(dimension_semantics=sem, vmem_limit_bytes=VMEM_LIMIT)


def _rms(x):
    return x * lax.rsqrt(jnp.mean(x * x, axis=-1, keepdims=True) + RMS_EPS)


def _softplus(x):
    return jnp.maximum(x, 0.0) + jnp.log(1.0 + jnp.exp(-jnp.abs(x)))


def _silu(x):
    return x * (1.0 / (1.0 + jnp.exp(-x)))


def _dot(a, b):
    return jnp.dot(a, b, preferred_element_type=F32)


def _dot_nt(a, b):
    return lax.dot_general(a, b, (((1,), (1,)), ((), ())), preferred_element_type=F32)


def _dot_tn(a, b):
    return lax.dot_general(a, b, (((0,), (0,)), ((), ())), preferred_element_type=F32)


def _split3(x):
    hi = x.astype(BF16)
    r1 = x - hi.astype(F32)
    mid = r1.astype(BF16)
    lo = (r1 - mid.astype(F32)).astype(BF16)
    return hi, mid, lo


def _dot_exact_rhs(m, x):
    hi, mid, lo = _split3(x)
    return _dot(m, hi) + _dot(m, mid) + _dot(m, lo)


def _dot_exact_lhs(x, m):
    hi, mid, lo = _split3(x)
    return _dot(hi, m) + _dot(mid, m) + _dot(lo, m)


def _const_spec(shape):
    return pl.BlockSpec(shape, lambda *_: (0,) * len(shape))


def _in_proj_kernel(x_ref, g_ref, wqkv_ref, wz_ref, wxbc_ref, wdt_ref,
                    q_ref, kf_ref, vf_ref, kb_ref, vb_ref, z_ref, xbc_ref, dt_ref):
    u = (_rms(x_ref[...]) * g_ref[...]).astype(BF16)
    qkv = _dot(u, wqkv_ref[...])
    q_ref[...] = (qkv[:, :ATT_WIDTH] * (ATT_HEAD_DIM ** -0.5)).astype(q_ref.dtype)
    k = qkv[:, ATT_WIDTH:2 * ATT_WIDTH]
    v = qkv[:, 2 * ATT_WIDTH:]
    kf_ref[...] = k
    vf_ref[...] = v
    kb_ref[...] = k.astype(BF16)
    vb_ref[...] = v.astype(BF16)
    z_ref[...] = _dot(u, wz_ref[...])
    xbc_ref[...] = _dot(u, wxbc_ref[...])
    dt_ref[...] = _dot(u, wdt_ref[...])


def _in_proj(x, g, wqkv, wz, wxbc, wdt, q_dtype):
    rows, d = x.shape
    tm = min(IN_PROJ_TILE, rows)
    row = lambda n: pl.BlockSpec((tm, n), lambda i: (i, 0))
    res = lambda a: pl.BlockSpec(a.shape, lambda i: (0, 0), pipeline_mode=pl.Buffered(1))
    out_shape = (
        jax.ShapeDtypeStruct((rows, ATT_WIDTH), q_dtype),
        jax.ShapeDtypeStruct((rows, ATT_WIDTH), F32),
        jax.ShapeDtypeStruct((rows, ATT_WIDTH), F32),
        jax.ShapeDtypeStruct((rows, ATT_WIDTH), BF16),
        jax.ShapeDtypeStruct((rows, ATT_WIDTH), BF16),
        jax.ShapeDtypeStruct((rows, SSD_WIDTH), F32),
        jax.ShapeDtypeStruct((rows, CONV_DIM), F32),
        jax.ShapeDtypeStruct((rows, LANES), F32),
    )
    return pl.pallas_call(
        _in_proj_kernel,
        grid=(rows // tm,),
        in_specs=[row(d), res(g), res(wqkv), res(wz), res(wxbc), res(wdt)],
        out_specs=tuple(row(s.shape[1]) for s in out_shape),
        out_shape=out_shape,
        compiler_params=_params("parallel"),
        name="in_proj",
    )(x, g, wqkv, wz, wxbc, wdt)


def _stick_block(z, carry, tri, mask):
    rows, tk = z.shape
    sp = _softplus(z)
    if mask is not None:
        sp = jnp.where(mask, sp, 0.0)
    cs = _dot(sp.astype(BF16), tri)
    wide = carry if tk == LANES else jnp.concatenate([carry] * (tk // LANES), axis=1)
    p = jnp.exp(z - cs[:, :tk] - wide)
    if mask is not None:
        p = jnp.where(mask, p, 0.0)
    total = cs[:, tk:] if tri.shape[1] > tk else jnp.broadcast_to(cs[:, 0:1], (rows, LANES))
    return p, carry + total


def _tri(tk, with_ones):
    n = tk + LANES if with_ones else tk
    s = lax.broadcasted_iota(jnp.int32, (tk, n), 0)
    j = lax.broadcasted_iota(jnp.int32, (tk, n), 1)
    return ((s >= j) | (j >= tk)).astype(BF16)


def _prompt_stages(qi, q_ref, qfill_ref, kfill_ref, k_ref, v_ref, tri_ref, acc_ref, c_ref, z_ref, p_ref):
    tq, tk = ATT_TQ, ATT_TK
    n_blocks = (qi + 1) * (tq // tk)
    low = lax.broadcasted_iota(jnp.int32, (1, LANES), 1) < ATT_HEAD_DIM
    own = (low, jnp.logical_not(low))
    q = q_ref[...]
    qs = tuple(jnp.where(own[a], q, qfill_ref[a]) for a in range(2))
    tri = tri_ref[...]
    q_pos = qi * tq + lax.broadcasted_iota(jnp.int32, (tq, tk), 0)
    k_off = lax.broadcasted_iota(jnp.int32, (tq, tk), 1)

    def rows(n):
        return pl.ds(pl.multiple_of(jnp.maximum(n_blocks - 1 - n, 0) * tk, tk), tk)

    def scores(n, slot):
        kb = k_ref[rows(n), :]
        for a in range(2):
            z_ref[slot, a] = _dot_nt(qs[a], jnp.where(own[a], kb, kfill_ref[a]))

    def weights(n, slot, masked):
        mask = ((n_blocks - 1 - n) * tk + k_off) < q_pos if masked else None
        for a in range(2):
            p, c_ref[a] = _stick_block(z_ref[slot, a], c_ref[a], tri, mask)
            p_ref[slot, :, a * tk:(a + 1) * tk] = p.astype(BF16)

    def apply(n, slot):
        vb = v_ref[rows(n), :]
        vz = jnp.zeros_like(vb)
        vcat = jnp.concatenate([jnp.where(low, vb, vz), jnp.where(low, vz, vb)], axis=0)
        acc_ref[...] += _dot(p_ref[slot], vcat)

    def pair(u, masked, first):
        n = 2 * u
        scores(n + 1, 1)
        if not first:
            apply(n - 1, 1)
        weights(n, 0, masked)
        scores(n + 2, 0)
        apply(n, 0)
        weights(n + 1, 1, masked)

    def start():
        acc_ref[...] = jnp.zeros_like(acc_ref)
        c_ref[...] = jnp.zeros_like(c_ref)
        scores(0, 0)

    def finish(o_ref):
        apply(n_blocks - 1, 1)
        o_ref[...] = acc_ref[...].astype(o_ref.dtype)

    return start, pair, finish


def _bias_lanes(bias):
    b = bias.astype(F32)
    b0 = b.astype(BF16)
    r = b - b0.astype(F32)
    b1 = r.astype(BF16)
    b2 = (r - b1.astype(F32)).astype(BF16)
    terms = jnp.stack([b0, b1, b2], axis=-1).reshape(HEAD_PAIRS, 2, 3)
    pad = jnp.zeros((HEAD_PAIRS, ATT_HEAD_DIM - 3), BF16)
    zero = jnp.zeros((HEAD_PAIRS, ATT_HEAD_DIM), BF16)
    first = jnp.concatenate([zero, terms[:, 0], pad], axis=1)
    second = jnp.concatenate([terms[:, 1], pad, zero], axis=1)
    qfill = jnp.stack([first, second], axis=1)
    ones = jnp.concatenate([jnp.ones((3,), BF16), jnp.zeros((ATT_HEAD_DIM - 3,), BF16)])
    zeros = jnp.zeros((ATT_HEAD_DIM,), BF16)
    kfill = jnp.stack([jnp.concatenate([zeros, ones]), jnp.concatenate([ones, zeros])])
    return qfill, kfill


def _sample_stages(q_ref, kn_ref, vn_ref, brow_ref, tri_ref, k_refs, v_refs, qexp_ref, acc_ref, c_ref):
    n = len(k_refs)
    rows = ATT_HEADS * q_ref.shape[0]
    lq = q_ref.shape[0]
    col_head = lax.broadcasted_iota(jnp.int32, (1, ATT_WIDTH), 1) // ATT_HEAD_DIM
    tri = tri_ref[...]
    brow = brow_ref[...]

    def begin():
        qt = jnp.concatenate([q_ref[...]] * ATT_HEADS, axis=0)
        row_head = lax.broadcasted_iota(jnp.int32, (rows, 1), 0) // lq
        qexp_ref[...] = jnp.where(row_head == col_head, qt, 0.0).astype(BF16)
        c_ref[...] = jnp.zeros_like(c_ref)
        pad = jnp.zeros((PAGE_SIZE - lq, ATT_WIDTH), F32)
        kn = jnp.concatenate([kn_ref[...], pad], axis=0)
        vn = jnp.concatenate([vn_ref[...], pad], axis=0)
        t = lax.broadcasted_iota(jnp.int32, (rows, PAGE_SIZE), 0) % lq
        tok = lax.broadcasted_iota(jnp.int32, (rows, PAGE_SIZE), 1)
        z = _dot_nt(qexp_ref[...], kn.astype(BF16)) + brow
        p, c_ref[...] = _stick_block(z, jnp.zeros_like(brow), tri, tok < t)
        acc_ref[...] = _dot(p.astype(BF16), vn.astype(BF16))

    def pages():
        kcat = jnp.concatenate([k_refs[i][...].astype(BF16) for i in reversed(range(n))], axis=1)
        vcat = jnp.concatenate([v_refs[i][...].astype(BF16) for i in reversed(range(n))], axis=1)
        z = _dot(qexp_ref[...], kcat)
        c = c_ref[...]
        ps = [None] * n
        for i in reversed(range(n)):
            zi = z[:, i * PAGE_SIZE:(i + 1) * PAGE_SIZE] + brow
            p, c = _stick_block(zi, c, tri, None)
            ps[i] = p.astype(BF16)
        c_ref[...] = c
        acc_ref[...] += _dot_nt(jnp.concatenate(ps, axis=1), vcat)

    def end(o_ref):
        out = jnp.zeros((lq, ATT_WIDTH), F32)
        for h in range(ATT_HEADS):
            out = jnp.where(col_head == h, acc_ref[h * lq:(h + 1) * lq, :], out)
        o_ref[...] = out

    return begin, pages, end


def _attn_kernel(hp_tab, qi_tab, u_tab, pt_ref,
                 q_ref, qfill_ref, kfill_ref, k_ref, v_ref, trip_ref,
                 qs_ref, kn_ref, vn_ref, brow_ref, tris_ref, *rest):
    n = PAGES_PER_STEP
    k_refs, v_refs = rest[:n], rest[n:2 * n]
    op_ref, os_ref, acc_ref, c_ref, z_ref, p_ref, qexp_ref, sacc_ref, sc_ref = rest[2 * n:]
    s = pl.program_id(0)
    qi = qi_tab[s]
    u = u_tab[s]
    per_seq = pt_ref.shape[1] // n
    page_steps = pt_ref.shape[0] * per_seq
    live = s < page_steps
    w = jnp.minimum(s, page_steps - 1) % per_seq
    start, pair, finish = _prompt_stages(qi, q_ref, qfill_ref, kfill_ref, k_ref, v_ref, trip_ref,
                                         acc_ref, c_ref, z_ref, p_ref)
    begin, pages, end = _sample_stages(qs_ref, kn_ref, vn_ref, brow_ref, tris_ref, k_refs, v_refs,
                                       qexp_ref, sacc_ref, sc_ref)

    @pl.when(jnp.logical_and(live, w == 0))
    def _():
        begin()

    @pl.when(u == 0)
    def _():
        start()
        pair(0, True, True)
        pages()

    @pl.when(u > 0)
    def _():
        pair(u, False, False)
        pages()

    @pl.when(u == qi)
    def _():
        finish(op_ref)

    @pl.when(jnp.logical_and(live, w == per_seq - 1))
    def _():
        end(os_ref)


def _attention(bias, page_table, q_p, k_p, v_p, q_s, k_new, v_new, cache_k, cache_v):
    l = q_p.shape[0]
    db, lq, _ = q_s.shape
    n_pages = page_table.shape[1]
    n = PAGES_PER_STEP
    tq, tk = ATT_TQ, ATT_TK
    nq = l // tq
    assert tq == 2 * tk and l % tq == 0 and n_pages % n == 0
    per_seq = n_pages // n
    page_steps = db * per_seq
    steps = [(hp, qi, u) for hp in range(HEAD_PAIRS) for qi in range(nq) for u in range(qi + 1)]
    assert len(steps) >= page_steps, "the sample page stream needs at least one grid step per page step"
    hp_tab, qi_tab, u_tab = (jnp.asarray([t[i] for t in steps], jnp.int32) for i in range(3))
    qfill, kfill = _bias_lanes(bias)
    qfill = jnp.broadcast_to(qfill[:, :, None, :], (HEAD_PAIRS, 2, tq, LANES))
    kfill = jnp.broadcast_to(kfill[:, None, :], (2, tk, LANES))
    rows = ATT_HEADS * lq
    pool = cache_k.shape[0]
    ck = jnp.transpose(cache_k, (0, 2, 3, 1)).reshape(pool, ATT_WIDTH, PAGE_SIZE)
    cv = jnp.transpose(cache_v, (0, 2, 3, 1)).reshape(pool, ATT_WIDTH, PAGE_SIZE)
    brow = jnp.broadcast_to(jnp.repeat(bias.astype(F32), lq)[:, None], (rows, LANES))

    def page_step(s):
        return jnp.minimum(s, page_steps - 1)

    const = lambda shape: pl.BlockSpec(shape, lambda s, hp, qi, u, pt: (0,) * len(shape))
    tile = pl.BlockSpec((tq, LANES), lambda s, hp, qi, u, pt: (qi[s], hp[s]))
    whole = pl.BlockSpec((l, LANES), lambda s, hp, qi, u, pt: (0, hp[s]))
    seq = pl.BlockSpec((None, lq, ATT_WIDTH), lambda s, hp, qi, u, pt: (page_step(s) // per_seq, 0, 0))

    def page(i):
        def index(s, hp, qi, u, pt):
            ps = page_step(s)
            return pt[ps // per_seq, n_pages - 1 - ((ps % per_seq) * n + i)], 0, 0
        return pl.BlockSpec((None, ATT_WIDTH, PAGE_SIZE), index)

    grid_spec = pltpu.PrefetchScalarGridSpec(
        num_scalar_prefetch=4,
        grid=(len(steps),),
        in_specs=[tile,
                  pl.BlockSpec((None, 2, tq, LANES), lambda s, hp, qi, u, pt: (hp[s], 0, 0, 0)),
                  const((2, tk, LANES)), whole, whole, const((tk, tk)),
                  seq, seq, seq, const((rows, LANES)), const((PAGE_SIZE, PAGE_SIZE + LANES))]
                 + [page(i) for i in range(n)] * 2,
        out_specs=(tile, seq),
        scratch_shapes=[pltpu.VMEM((tq, LANES), F32), pltpu.VMEM((2, tq, LANES), F32),
                        pltpu.VMEM((2, 2, tq, tk), F32), pltpu.VMEM((2, tq, 2 * tk), BF16),
                        pltpu.VMEM((rows, ATT_WIDTH), BF16), pltpu.VMEM((rows, ATT_WIDTH), F32),
                        pltpu.VMEM((rows, LANES), F32)],
    )
    return pl.pallas_call(
        _attn_kernel,
        grid_spec=grid_spec,
        out_shape=(jax.ShapeDtypeStruct((l, ATT_WIDTH), BF16), jax.ShapeDtypeStruct((db, lq, ATT_WIDTH), F32)),
        compiler_params=_params("arbitrary"),
        name="attention",
    )(hp_tab, qi_tab, u_tab, page_table, q_p, qfill, kfill, k_p, v_p, _tri(tk, False),
      q_s, k_new, v_new, brow, _tri(PAGE_SIZE, True), *([ck] * n), *([cv] * n))


def _ssd_kernel(xbc_ref, z_ref, dtr_ref, s0_ref, c0_ref, wconv_ref, bconv_ref, dtb_ref, alog_ref,
                dskip_ref, gn_ref, tril_ref, e_ref,
                y_ref, sn_ref, cn_ref, cbuf_ref, state_ref):
    c = pl.program_id(1)
    cs = xbc_ref.shape[0]
    ck = SSD_CHUNK
    tail = 8
    hist = SSD_CONV - 1

    @pl.when(c == 0)
    def _():
        state_ref[...] = s0_ref[...]
        cbuf_ref[0:tail, :] = jnp.zeros((tail, CONV_DIM), F32)
        cbuf_ref[tail - hist:tail, :] = c0_ref[...]

    cbuf_ref[tail:tail + cs, :] = xbc_ref[...]
    w = wconv_ref[...]
    xc = bconv_ref[...] + cbuf_ref[tail - hist:tail - hist + cs, :] * w[0:1]
    for j in range(1, SSD_CONV):
        xc = xc + cbuf_ref[tail - hist + j:tail - hist + j + cs, :] * w[j:j + 1]
    cn_ref[...] = cbuf_ref[tail + cs - hist:tail + cs, :]
    last = cbuf_ref[cs:cs + tail, :]
    cbuf_ref[0:tail, :] = last
    xc = _silu(xc)
    dt = _softplus(dtr_ref[...] + dtb_ref[...])
    zg = z_ref[...]
    if cs < ck:
        xc = jnp.concatenate([xc, jnp.zeros((ck - cs, CONV_DIM), F32)], axis=0)
        dt = jnp.concatenate([dt, jnp.zeros((ck - cs, LANES), F32)], axis=0)
        zg = jnp.concatenate([zg, jnp.zeros((ck - cs, SSD_WIDTH), F32)], axis=0)

    a = -jnp.exp(alog_ref[...])
    acum = _dot_exact_rhs(tril_ref[...], dt * a)
    acum_row = acum.T
    e = e_ref[...]
    dt_x = _dot_exact_lhs(dt, e)
    ac_x = _dot_exact_lhs(acum, e)
    xs = xc[:, :SSD_WIDTH]
    xd = xs * dt_x
    xdw = (xd * jnp.exp(ac_x[ck - 1:ck, :] - ac_x)).astype(BF16)
    eac_x = jnp.exp(ac_x)
    row_i = lax.broadcasted_iota(jnp.int32, (ck, ck), 0)
    col_j = lax.broadcasted_iota(jnp.int32, (ck, ck), 1)
    causal = row_i >= col_j
    low = lax.broadcasted_iota(jnp.int32, (1, LANES), 1) < SSD_HEAD_DIM
    per_group = SSD_HEADS // SSD_GROUPS
    ys = []
    gmat = {}
    for hp in range(SSD_WIDTH // LANES):
        g = (2 * hp) // per_group
        b_g = xc[:, SSD_WIDTH + g * SSD_STATE:SSD_WIDTH + (g + 1) * SSD_STATE].astype(BF16)
        c_off = SSD_WIDTH + SSD_GROUPS * SSD_STATE
        c_g = xc[:, c_off + g * SSD_STATE:c_off + (g + 1) * SSD_STATE].astype(BF16)
        if g not in gmat:
            gmat[g] = _dot_nt(c_g, b_g)
        sl = slice(hp * LANES, (hp + 1) * LANES)
        xd_p = xd[:, sl]
        y_p = None
        decs = []
        for k in range(2):
            h = 2 * hp + k
            diff = acum[:, h:h + 1] - acum_row[h:h + 1, :]
            decay = jnp.where(causal, jnp.exp(jnp.minimum(diff, 0.0)), 0.0)
            sc = (gmat[g] * decay).astype(BF16)
            xd_k = jnp.where(low if k == 0 else jnp.logical_not(low), xd_p, 0.0).astype(BF16)
            d = _dot(sc, xd_k)
            y_p = d if y_p is None else y_p + d
            decs.append(jnp.broadcast_to(jnp.exp(acum[ck - 1:ck, h:h + 1]), (SSD_HEAD_DIM, SSD_STATE)))
        s_p = state_ref[hp]
        y_p = y_p + _dot_nt(c_g, s_p.astype(BF16)) * eac_x[:, sl]
        state_ref[hp] = s_p * jnp.concatenate(decs, axis=0) + _dot_tn(xdw[:, sl], b_g)
        ys.append(y_p + dskip_ref[:, sl] * xs[:, sl])
    y = jnp.concatenate(ys, axis=1) * _silu(zg)
    y = _rms(y) * gn_ref[...]
    y_ref[...] = y[:cs, :]

    @pl.when(c == pl.num_programs(1) - 1)
    def _():
        sn_ref[...] = state_ref[...]


def _ssd(xbc, z, dtr, s0, c0, wconv, bconv, dtb, alog, dskip, gn, batch):
    rows = xbc.shape[0]
    l = rows // batch
    cs = min(SSD_CHUNK, l)
    nc = l // cs
    pairs = SSD_WIDTH // LANES
    s0 = s0.reshape(batch, pairs, 2 * SSD_HEAD_DIM, SSD_STATE)
    i = lax.broadcasted_iota(jnp.int32, (SSD_CHUNK, SSD_CHUNK), 0)
    j = lax.broadcasted_iota(jnp.int32, (SSD_CHUNK, SSD_CHUNK), 1)
    tril = (j <= i).astype(BF16)
    eh = lax.broadcasted_iota(jnp.int32, (LANES, SSD_WIDTH), 0)
    ec = lax.broadcasted_iota(jnp.int32, (LANES, SSD_WIDTH), 1) // SSD_HEAD_DIM
    expand = (eh == ec).astype(BF16)
    row = lambda n: pl.BlockSpec((cs, n), lambda b, c: (b * nc + c, 0))
    vec = lambda a: pl.BlockSpec(a.shape, lambda b, c: (0, 0))
    st = pl.BlockSpec((None, pairs, 2 * SSD_HEAD_DIM, SSD_STATE), lambda b, c: (b, 0, 0, 0))
    cv = pl.BlockSpec((None, SSD_CONV - 1, CONV_DIM), lambda b, c: (b, 0, 0))
    y, sn, cn = pl.pallas_call(
        _ssd_kernel,
        grid=(batch, nc),
        in_specs=[row(CONV_DIM), row(SSD_WIDTH), row(LANES), st, cv,
                  vec(wconv), vec(bconv), vec(dtb), vec(alog), vec(dskip), vec(gn), vec(tril), vec(expand)],
        out_specs=(row(SSD_WIDTH), st, cv),
        out_shape=(jax.ShapeDtypeStruct((rows, SSD_WIDTH), F32),
                   jax.ShapeDtypeStruct(s0.shape, F32),
                   jax.ShapeDtypeStruct((batch, SSD_CONV - 1, CONV_DIM), F32)),
        scratch_shapes=[pltpu.VMEM((8 + cs, CONV_DIM), F32),
                        pltpu.VMEM((pairs, 2 * SSD_HEAD_DIM, SSD_STATE), F32)],
        compiler_params=_params("parallel", "arbitrary"),
        name="ssd",
    )(xbc, z, dtr, s0, c0, wconv, bconv, dtb, alog, dskip, gn, tril, expand)
    return y, sn.reshape(batch, SSD_HEADS, SSD_HEAD_DIM, SSD_STATE), cn


def _out_proj_kernel(o_ref, y_ref, x_ref, woa_ref, wob_ref, gpost_ref, gpre_ref, h_ref, hn_ref):
    mix = _dot(o_ref[...].astype(BF16), woa_ref[...]) + _dot(y_ref[...].astype(BF16), wob_ref[...])
    h = x_ref[...] + _rms(mix) * gpost_ref[...]
    h_ref[...] = h
    hn_ref[...] = (_rms(h) * gpre_ref[...]).astype(BF16)


def _out_proj(o, y, x, woa, wob, gpost, gpre):
    rows, d = x.shape
    tm = min(ROW_TILE, rows)
    row = lambda n: pl.BlockSpec((tm, n), lambda i: (i, 0))
    return pl.pallas_call(
        _out_proj_kernel,
        grid=(rows // tm,),
        in_specs=[row(ATT_WIDTH), row(SSD_WIDTH), row(d),
                  _const_spec(woa.shape), _const_spec(wob.shape), _const_spec(gpost.shape), _const_spec(gpre.shape)],
        out_specs=(row(d), row(d)),
        out_shape=(jax.ShapeDtypeStruct((rows, d), F32), jax.ShapeDtypeStruct((rows, d), BF16)),
        compiler_params=_params("parallel"),
        name="out_proj",
    )(o, y, x, woa, wob, gpost, gpre)


def _gelu_tanh(x):
    return 0.5 * x * (1.0 + jnp.tanh(0.7978845608028654 * (x + 0.044715 * (x * x * x))))


def _ffn_kernel(*refs, seq_len):
    if seq_len is None:
        hn_ref, halo_ref, wg_ref, wv_ref, cwg_ref, cwv_ref, cbg_ref, cbv_ref, wd_ref, f_ref, ug_ref, uv_ref, xe_ref = refs
    else:
        hn_ref, stg_ref, stv_ref, wg_ref, wv_ref, cwg_ref, cwv_ref, cbg_ref, cbv_ref, wd_ref, f_ref, ug_ref, uv_ref = refs
    i = pl.program_id(0)
    j = pl.program_id(1)
    tm = hn_ref.shape[0]

    @pl.when(j == 0)
    def _():
        f_ref[...] = jnp.zeros_like(f_ref)

    if seq_len is None:
        @pl.when(j == 0)
        def _():
            halo = halo_ref[...]
            xe_ref[0:FFN_HALO, :] = jnp.where(i > 0, halo, jnp.zeros_like(halo))
            xe_ref[FFN_HALO:, :] = hn_ref[...]

        def conv(w_ref, cw_ref, cb_ref, u_ref):
            up = _dot(xe_ref[...], w_ref[...])
            cw = cw_ref[...]
            y = (cb_ref[...] + pltpu.roll(up, 2, 0)[FFN_HALO:] * cw[0:1]
                 + pltpu.roll(up, 1, 0)[FFN_HALO:] * cw[1:2] + up[FFN_HALO:] * cw[2:3])
            u_ref[...] = up[FFN_HALO + tm - 8:]
            return y
        gate = conv(wg_ref, cwg_ref, cbg_ref, ug_ref)
        val = conv(wv_ref, cwv_ref, cbv_ref, uv_ref)
    else:
        t = lax.broadcasted_iota(jnp.int32, (tm, 1), 0) % seq_len

        def conv(w_ref, cw_ref, cb_ref, st_ref, u_ref):
            up = _dot(hn_ref[...], w_ref[...])
            st = st_ref[...]
            cw = cw_ref[...]
            prev2 = jnp.where(t < 2, st, pltpu.roll(up, 2, 0))
            prev1 = jnp.where(t < 1, pltpu.roll(st, tm - 1, 0), pltpu.roll(up, 1, 0))
            u_ref[...] = up
            return cb_ref[...] + prev2 * cw[0:1] + prev1 * cw[1:2] + up * cw[2:3]
        gate = conv(wg_ref, cwg_ref, cbg_ref, stg_ref, ug_ref)
        val = conv(wv_ref, cwv_ref, cbv_ref, stv_ref, uv_ref)

    f_ref[...] += _dot((_gelu_tanh(gate) * val).astype(BF16), wd_ref[...])


def _ffn(hn, w_up, cw, cb, w_down, state=None, seq_len=None):
    rows, d = hn.shape
    dff = w_down.shape[0]
    tf = FFN_TF
    nj = dff // tf
    tm = min(FFN_TM, rows)
    gcol = lambda r: pl.BlockSpec((r, tf), lambda i, j: (0, j))
    vcol = lambda r: pl.BlockSpec((r, tf), lambda i, j: (0, j + nj))
    hn_spec = pl.BlockSpec((tm, d), lambda i, j: (i, 0))
    common = [pl.BlockSpec((d, tf), lambda i, j: (0, j)), pl.BlockSpec((d, tf), lambda i, j: (0, j + nj)),
              gcol(FFN_CONV), vcol(FFN_CONV), gcol(1), vcol(1),
              pl.BlockSpec((tf, d), lambda i, j: (j, 0))]
    common_args = (w_up, w_up, cw, cw, cb, cb, w_down)
    f_spec = pl.BlockSpec((tm, d), lambda i, j: (i, 0))
    if seq_len is None:
        per = tm // FFN_HALO
        halo = pl.BlockSpec((FFN_HALO, d), lambda i, j: (jnp.maximum(i * per - 1, 0), 0))
        in_specs = [hn_spec, halo] + common
        args = (hn, hn) + common_args
        u_rows = 8 * (rows // tm)
        u_spec = pl.BlockSpec((8, tf), lambda i, j: (i, j))
        scratch = [pltpu.VMEM((FFN_HALO + tm, d), BF16)]
    else:
        assert rows == tm
        in_specs = [hn_spec, pl.BlockSpec((tm, tf), lambda i, j: (0, j)),
                    pl.BlockSpec((tm, tf), lambda i, j: (0, j + nj))] + common
        args = (hn, state, state) + common_args
        u_rows = tm
        u_spec = pl.BlockSpec((u_rows, tf), lambda i, j: (0, j))
        scratch = []
    return pl.pallas_call(
        functools.partial(_ffn_kernel, seq_len=seq_len),
        grid=(rows // tm, nj),
        in_specs=in_specs,
        out_specs=(f_spec, u_spec, u_spec),
        out_shape=(jax.ShapeDtypeStruct((rows, d), F32),
                   jax.ShapeDtypeStruct((u_rows, dff), F32),
                   jax.ShapeDtypeStruct((u_rows, dff), F32)),
        scratch_shapes=scratch,
        compiler_params=_params("arbitrary", "arbitrary"),
        name="ffn",
    )(*args)


def _ple_kernel(h_ref, f_ref, p_ref, gpost_ref, wg_ref, wp_ref, o_ref):
    h = h_ref[...] + _rms(f_ref[...]) * gpost_ref[...]
    gate = 1.0 / (1.0 + jnp.exp(-_dot(h.astype(BF16), wg_ref[...])))
    o_ref[...] = h + _dot(p_ref[...].astype(BF16), wp_ref[...]) * gate


def _ple(h, f, p, gpost, wg, wp):
    rows, d = h.shape
    tm = min(ROW_TILE, rows)
    row = lambda n: pl.BlockSpec((tm, n), lambda i: (i, 0))
    return pl.pallas_call(
        _ple_kernel,
        grid=(rows // tm,),
        in_specs=[row(d), row(d), row(p.shape[1]),
                  _const_spec(gpost.shape), _const_spec(wg.shape), _const_spec(wp.shape)],
        out_specs=row(d),
        out_shape=jax.ShapeDtypeStruct((rows, d), F32),
        compiler_params=_params("parallel"),
        name="ple",
    )(h, f, p, gpost, wg, wp)


def _after_attention(x, p, lw, batch, o, kf, vf, z, xbc, dtr, ssm0, conv0, ffn0):
    rows, d = x.shape
    l = rows // batch
    sample = ffn0 is not None
    y, ssm_new, conv_new = _ssd(xbc, z, dtr, ssm0, conv0, lw["w_conv"], lw["b_conv"], lw["dt_bias"], lw["a_log"],
                                lw["d_skip"], lw["g_ssd_norm"], batch)
    h1, hn = _out_proj(o, y, x, lw["w_out_a"], lw["w_out_b"], lw["g_post_mix"], lw["g_pre_ffn"])
    dff2 = lw["w_up"].shape[1]
    if sample:
        st = jnp.pad(ffn0, ((0, 0), (0, l - (FFN_CONV - 1)), (0, 0))).reshape(rows, dff2)
        f, ug, uv = _ffn(hn, lw["w_up"], lw["w_ffn_conv"], lw["b_ffn_conv"], lw["w_down"], state=st, seq_len=l)
        up = jnp.concatenate([ug, uv], axis=1).reshape(batch, l, dff2)
        ffn_new = up[:, l - (FFN_CONV - 1):, :]
    else:
        f, ug, uv = _ffn(hn, lw["w_up"], lw["w_ffn_conv"], lw["b_ffn_conv"], lw["w_down"])
        ffn_new = jnp.concatenate([ug, uv], axis=1)[None, -(FFN_CONV - 1):, :]
    out = _ple(h1, f, p, lw["g_post_ffn"], lw["w_ple_gate"], lw["w_ple_proj"])
    return out, (kf, vf, ssm_new, conv_new, ffn_new)


def _prep_weights(i, g_pre_mix, w_in, sb_bias, w_conv, b_conv, dt_bias, a_log, d_skip, g_ssd_norm, w_out,
                  g_post_mix, g_pre_ffn, w_up, w_ffn_conv, b_ffn_conv, w_down, g_post_ffn, w_ple_gate, w_ple_proj):
    row = lambda a: a[i][None, :].astype(F32)
    lane_pad = lambda a: jnp.pad(a[i].astype(F32), (0, LANES - a.shape[1]))[None, :]
    w = w_in[i].astype(BF16)
    o_z = 3 * ATT_WIDTH
    o_x = o_z + SSD_WIDTH
    o_dt = o_x + CONV_DIM
    return {
        "g_pre_mix": row(g_pre_mix),
        "w_qkv": w[:, :o_z], "w_z": w[:, o_z:o_x], "w_xbc": w[:, o_x:o_dt],
        "w_dt": jnp.pad(w[:, o_dt:], ((0, 0), (0, LANES - SSD_HEADS))),
        "sb_bias": sb_bias[i].astype(F32),
        "w_conv": w_conv[i].astype(F32), "b_conv": row(b_conv),
        "dt_bias": lane_pad(dt_bias), "a_log": lane_pad(a_log),
        "d_skip": jnp.repeat(d_skip[i].astype(F32), SSD_HEAD_DIM)[None, :],
        "g_ssd_norm": row(g_ssd_norm),
        "w_out_a": w_out[i, :ATT_WIDTH].astype(BF16), "w_out_b": w_out[i, ATT_WIDTH:].astype(BF16),
        "g_post_mix": row(g_post_mix), "g_pre_ffn": row(g_pre_ffn),
        "w_up": w_up[i].astype(BF16), "w_ffn_conv": w_ffn_conv[i].astype(F32), "b_ffn_conv": row(b_ffn_conv),
        "w_down": w_down[i].astype(BF16), "g_post_ffn": row(g_post_ffn),
        "w_ple_gate": w_ple_gate[i].astype(BF16), "w_ple_proj": w_ple_proj[i].astype(BF16),
    }


def kernel(x_prompt, x_sample, cache_k, cache_v, state_ssm, state_conv, state_ffn_conv, page_table, p_prompt, p_sample, g_pre_mix, w_in, sb_bias, w_conv, b_conv, dt_bias, a_log, d_skip, g_ssd_norm, w_out, g_post_mix, g_pre_ffn, w_up, w_ffn_conv, b_ffn_conv, w_down, g_post_ffn, w_ple_gate, w_ple_proj):
    bp, lp, d = x_prompt.shape
    db, ls, _ = x_sample.shape
    depth = w_in.shape[0]
    dff2 = w_up.shape[2]
    hp = x_prompt.reshape(bp * lp, d)
    hs = x_sample.reshape(db * ls, d)
    outs_p, outs_s = [], []
    for i in range(depth):
        lw = _prep_weights(i, g_pre_mix, w_in, sb_bias, w_conv, b_conv, dt_bias, a_log, d_skip, g_ssd_norm, w_out,
                           g_post_mix, g_pre_ffn, w_up, w_ffn_conv, b_ffn_conv, w_down, g_post_ffn,
                           w_ple_gate, w_ple_proj)
        ssm0 = jnp.zeros((bp, SSD_HEADS, SSD_HEAD_DIM, SSD_STATE), F32)
        conv0 = jnp.zeros((bp, SSD_CONV - 1, CONV_DIM), F32)
        proj = lambda x, q_dtype: _in_proj(x, lw["g_pre_mix"], lw["w_qkv"], lw["w_z"], lw["w_xbc"], lw["w_dt"], q_dtype)
        q_p, kf_p, vf_p, kb_p, vb_p, z_p, xbc_p, dt_p = proj(hp, BF16)
        q_s, kf_s, vf_s, _, _, z_s, xbc_s, dt_s = proj(hs, F32)
        seq = lambda a: a.reshape(db, ls, ATT_WIDTH)
        o_p, o_s = _attention(lw["sb_bias"], page_table, q_p, kb_p, vb_p, seq(q_s), seq(kf_s), seq(vf_s),
                              cache_k[i], cache_v[i])
        hp, st_p = _after_attention(hp, p_prompt[i].reshape(bp * lp, -1), lw, bp, o_p, kf_p, vf_p, z_p, xbc_p, dt_p,
                                    ssm0, conv0, None)
        hs, st_s = _after_attention(hs, p_sample[i].reshape(db * ls, -1), lw, db, o_s.reshape(db * ls, ATT_WIDTH),
                                    kf_s, vf_s, z_s, xbc_s, dt_s, state_ssm[i], state_conv[i], state_ffn_conv[i])
        outs_p.append(st_p)
        outs_s.append(st_s)

    def stack(outs, j, shape):
        return jnp.stack([o[j].reshape(shape) for o in outs], axis=0)

    kv_p = (bp, lp, ATT_HEADS, ATT_HEAD_DIM)
    kv_s = (db, ls, ATT_HEADS, ATT_HEAD_DIM)
    ssm_p = (bp, SSD_HEADS, SSD_HEAD_DIM, SSD_STATE)
    ssm_s = (db, SSD_HEADS, SSD_HEAD_DIM, SSD_STATE)
    return (hp.reshape(bp, lp, d), hs.reshape(db, ls, d),
            stack(outs_p, 0, kv_p), stack(outs_s, 0, kv_s), stack(outs_p, 1, kv_p), stack(outs_s, 1, kv_s),
            stack(outs_p, 2, ssm_p), stack(outs_s, 2, ssm_s),
            stack(outs_p, 3, (bp, SSD_CONV - 1, CONV_DIM)), stack(outs_s, 3, (db, SSD_CONV - 1, CONV_DIM)),
            stack(outs_p, 4, (bp, FFN_CONV - 1, dff2)), stack(outs_s, 4, (db, FFN_CONV - 1, dff2)))
```

```python
import functools

import jax
import jax.numpy as jnp
import numpy as np
from jax import lax
from jax.experimental import pallas as pl
from jax.experimental.pallas import tpu as pltpu

F32 = jnp.float32
BF16 = jnp.bfloat16

ATT_HEADS = 16
ATT_HEAD_DIM = 64
ATT_WIDTH = ATT_HEADS * ATT_HEAD_DIM
SSD_HEADS = 16
SSD_HEAD_DIM = 64
SSD_WIDTH = SSD_HEADS * SSD_HEAD_DIM
SSD_GROUPS = 2
SSD_STATE = 128
SSD_CONV = 4
SSD_CHUNK = 128
CONV_DIM = SSD_WIDTH + 2 * SSD_GROUPS * SSD_STATE
FFN_CONV = 3
RMS_EPS = 1e-6
PAGE_SIZE = 128

LANES = 128
HEAD_PAIRS = ATT_WIDTH // LANES
VMEM_LIMIT = 56 * 1024 * 1024

ROW_TILE = 512
IN_PROJ_TILE = 256
ATT_TQ = 512
ATT_TK = 256
FFN_TM = 1024
FFN_TF = 512
FFN_HALO = 16
PAGES_PER_STEP = 4


def _params(*sem):
    return pltpu.CompilerParams(dimension_semantics=sem, vmem_limit_bytes=VMEM_LIMIT)


def _rms(x):
    return x * lax.rsqrt(jnp.mean(x * x, axis=-1, keepdims=True) + RMS_EPS)


def _softplus(x):
    return jnp.maximum(x, 0.0) + jnp.log(1.0 + jnp.exp(-jnp.abs(x)))


def _silu(x):
    return x * (1.0 / (1.0 + jnp.exp(-x)))


def _dot(a, b):
    return jnp.dot(a, b, preferred_element_type=F32)


def _dot_nt(a, b):
    return lax.dot_general(a, b, (((1,), (1,)), ((), ())), preferred_element_type=F32)


def _dot_tn(a, b):
    return lax.dot_general(a, b, (((0,), (0,)), ((), ())), preferred_element_type=F32)


def _split3(x):
    hi = x.astype(BF16)
    r1 = x - hi.astype(F32)
    mid = r1.astype(BF16)
    lo = (r1 - mid.astype(F32)).astype(BF16)
    return hi, mid, lo


def _dot_exact_rhs(m, x):
    hi, mid, lo = _split3(x)
    return _dot(m, hi) + _dot(m, mid) + _dot(m, lo)


def _dot_exact_lhs(x, m):
    hi, mid, lo = _split3(x)
    return _dot(hi, m) + _dot(mid, m) + _dot(lo, m)


def _const_spec(shape):
    return pl.BlockSpec(shape, lambda *_: (0,) * len(shape))


def _in_proj_kernel(x_ref, g_ref, w_ref, wdt_ref,
                    q_ref, kf_ref, vf_ref, kb_ref, vb_ref, z_ref, xbc_ref, dt_ref):
    o_z = 3 * ATT_WIDTH
    o_x = o_z + SSD_WIDTH
    u = (_rms(x_ref[...]) * g_ref[...]).astype(BF16)
    qkv = _dot(u, w_ref[:, :o_z])
    q_ref[...] = (qkv[:, :ATT_WIDTH] * (ATT_HEAD_DIM ** -0.5)).astype(q_ref.dtype)
    k = qkv[:, ATT_WIDTH:2 * ATT_WIDTH]
    v = qkv[:, 2 * ATT_WIDTH:]
    kf_ref[...] = k
    vf_ref[...] = v
    kb_ref[...] = k.astype(BF16)
    vb_ref[...] = v.astype(BF16)
    z_ref[...] = _dot(u, w_ref[:, o_z:o_x])
    xbc_ref[...] = _dot(u, w_ref[:, o_x:o_x + CONV_DIM])
    dt_ref[...] = _dot(u, wdt_ref[...])


def _in_proj(x, g, w, wdt, q_dtype):
    rows, d = x.shape
    tm = min(IN_PROJ_TILE, rows)
    row = lambda n: pl.BlockSpec((tm, n), lambda i: (i, 0))
    res = lambda a: pl.BlockSpec(a.shape, lambda i: (0, 0), pipeline_mode=pl.Buffered(1))
    out_shape = (
        jax.ShapeDtypeStruct((rows, ATT_WIDTH), q_dtype),
        jax.ShapeDtypeStruct((rows, ATT_WIDTH), F32),
        jax.ShapeDtypeStruct((rows, ATT_WIDTH), F32),
        jax.ShapeDtypeStruct((rows, ATT_WIDTH), BF16),
        jax.ShapeDtypeStruct((rows, ATT_WIDTH), BF16),
        jax.ShapeDtypeStruct((rows, SSD_WIDTH), F32),
        jax.ShapeDtypeStruct((rows, CONV_DIM), F32),
        jax.ShapeDtypeStruct((rows, LANES), F32),
    )
    return pl.pallas_call(
        _in_proj_kernel,
        grid=(rows // tm,),
        in_specs=[row(d), res(g), res(w), res(wdt)],
        out_specs=tuple(row(s.shape[1]) for s in out_shape),
        out_shape=out_shape,
        compiler_params=_params("parallel"),
        name="in_proj",
    )(x, g, w, wdt)


def _stick_block(z, carry, tri, mask):
    rows, tk = z.shape
    sp = _softplus(z)
    if mask is not None:
        sp = jnp.where(mask, sp, 0.0)
    cs = _dot(sp.astype(BF16), tri)
    wide = carry if tk == LANES else jnp.concatenate([carry] * (tk // LANES), axis=1)
    p = jnp.exp(z - cs[:, :tk] - wide)
    if mask is not None:
        p = jnp.where(mask, p, 0.0)
    total = cs[:, tk:] if tri.shape[1] > tk else jnp.broadcast_to(cs[:, 0:1], (rows, LANES))
    return p, carry + total


def _tri(tk, with_ones):
    n = tk + LANES if with_ones else tk
    s = lax.broadcasted_iota(jnp.int32, (tk, n), 0)
    j = lax.broadcasted_iota(jnp.int32, (tk, n), 1)
    return ((s >= j) | (j >= tk)).astype(BF16)


def _prompt_stages(qi, q_ref, qfill_ref, kfill_ref, k_ref, v_ref, tri_ref, acc_ref, c_ref, z_ref, p_ref):
    tq, tk = ATT_TQ, ATT_TK
    n_blocks = (qi + 1) * (tq // tk)
    low = lax.broadcasted_iota(jnp.int32, (1, LANES), 1) < ATT_HEAD_DIM
    own = (low, jnp.logical_not(low))
    q = q_ref[...]
    qs = tuple(jnp.where(own[a], q, qfill_ref[a]) for a in range(2))
    tri = tri_ref[...]
    q_pos = qi * tq + lax.broadcasted_iota(jnp.int32, (tq, tk), 0)
    k_off = lax.broadcasted_iota(jnp.int32, (tq, tk), 1)

    def rows(n):
        return pl.ds(pl.multiple_of(jnp.maximum(n_blocks - 1 - n, 0) * tk, tk), tk)

    def scores(n, slot):
        kb = k_ref[rows(n), :]
        for a in range(2):
            z_ref[slot, a] = _dot_nt(qs[a], jnp.where(own[a], kb, kfill_ref[a]))

    def weights(n, slot, masked):
        mask = ((n_blocks - 1 - n) * tk + k_off) < q_pos if masked else None
        for a in range(2):
            p, c_ref[a] = _stick_block(z_ref[slot, a], c_ref[a], tri, mask)
            p_ref[slot, :, a * tk:(a + 1) * tk] = p.astype(BF16)

    def apply(n, slot):
        vb = v_ref[rows(n), :]
        vz = jnp.zeros_like(vb)
        vcat = jnp.concatenate([jnp.where(low, vb, vz), jnp.where(low, vz, vb)], axis=0)
        acc_ref[...] += _dot(p_ref[slot], vcat)

    def pair(u, masked, first):
        n = 2 * u
        scores(n + 1, 1)
        if not first:
            apply(n - 1, 1)
        weights(n, 0, masked)
        scores(n + 2, 0)
        apply(n, 0)
        weights(n + 1, 1, masked)

    def start():
        acc_ref[...] = jnp.zeros_like(acc_ref)
        c_ref[...] = jnp.zeros_like(c_ref)
        scores(0, 0)

    def finish(o_ref):
        apply(n_blocks - 1, 1)
        o_ref[...] = acc_ref[...].astype(o_ref.dtype)

    return start, pair, finish


def _bias_lanes(bias):
    b = bias.astype(F32)
    b0 = b.astype(BF16)
    r = b - b0.astype(F32)
    b1 = r.astype(BF16)
    b2 = (r - b1.astype(F32)).astype(BF16)
    terms = jnp.stack([b0, b1, b2], axis=-1).reshape(HEAD_PAIRS, 2, 3)
    pad = jnp.zeros((HEAD_PAIRS, ATT_HEAD_DIM - 3), BF16)
    zero = jnp.zeros((HEAD_PAIRS, ATT_HEAD_DIM), BF16)
    first = jnp.concatenate([zero, terms[:, 0], pad], axis=1)
    second = jnp.concatenate([terms[:, 1], pad, zero], axis=1)
    qfill = jnp.stack([first, second], axis=1)
    ones = jnp.concatenate([jnp.ones((3,), BF16), jnp.zeros((ATT_HEAD_DIM - 3,), BF16)])
    zeros = jnp.zeros((ATT_HEAD_DIM,), BF16)
    kfill = jnp.stack([jnp.concatenate([zeros, ones]), jnp.concatenate([ones, zeros])])
    return qfill, kfill


def _sample_stages(q_ref, kn_ref, vn_ref, brow_ref, tri_ref, k_refs, v_refs, qexp_ref, acc_ref, c_ref):
    n = len(k_refs)
    rows = ATT_HEADS * q_ref.shape[0]
    lq = q_ref.shape[0]
    col_head = lax.broadcasted_iota(jnp.int32, (1, ATT_WIDTH), 1) // ATT_HEAD_DIM
    tri = tri_ref[...]
    brow = brow_ref[...]

    def begin():
        qt = jnp.concatenate([q_ref[...]] * ATT_HEADS, axis=0)
        row_head = lax.broadcasted_iota(jnp.int32, (rows, 1), 0) // lq
        qexp_ref[...] = jnp.where(row_head == col_head, qt, 0.0).astype(BF16)
        c_ref[...] = jnp.zeros_like(c_ref)
        pad = jnp.zeros((PAGE_SIZE - lq, ATT_WIDTH), F32)
        kn = jnp.concatenate([kn_ref[...], pad], axis=0)
        vn = jnp.concatenate([vn_ref[...], pad], axis=0)
        t = lax.broadcasted_iota(jnp.int32, (rows, PAGE_SIZE), 0) % lq
        tok = lax.broadcasted_iota(jnp.int32, (rows, PAGE_SIZE), 1)
        z = _dot_nt(qexp_ref[...], kn.astype(BF16)) + brow
        p, c_ref[...] = _stick_block(z, jnp.zeros_like(brow), tri, tok < t)
        acc_ref[...] = _dot(p.astype(BF16), vn.astype(BF16))

    def pages():
        kcat = jnp.concatenate([k_refs[i][...].astype(BF16) for i in reversed(range(n))], axis=1)
        vcat = jnp.concatenate([v_refs[i][...].astype(BF16) for i in reversed(range(n))], axis=1)
        z = _dot(qexp_ref[...], kcat)
        c = c_ref[...]
        ps = [None] * n
        for i in reversed(range(n)):
            zi = z[:, i * PAGE_SIZE:(i + 1) * PAGE_SIZE] + brow
            p, c = _stick_block(zi, c, tri, None)
            ps[i] = p.astype(BF16)
        c_ref[...] = c
        acc_ref[...] += _dot_nt(jnp.concatenate(ps, axis=1), vcat)

    def end(o_ref):
        out = jnp.zeros((lq, ATT_WIDTH), F32)
        for h in range(ATT_HEADS):
            out = jnp.where(col_head == h, acc_ref[h * lq:(h + 1) * lq, :], out)
        o_ref[...] = out

    return begin, pages, end


SEQ_FIRST, SEQ_MIDDLE, SEQ_LAST, SEQ_NONE = 0, 1, 2, 3


def _attn_kernel(hp_tab, qi_tab, u_tab, seq_tab, role_tab, page_tab,
                 q_ref, qfill_ref, kfill_ref, k_ref, v_ref, trip_ref,
                 qs_ref, kn_ref, vn_ref, brow_ref, tris_ref, *rest):
    n = PAGES_PER_STEP
    k_refs, v_refs = rest[:n], rest[n:2 * n]
    op_ref, os_ref, acc_ref, c_ref, z_ref, p_ref, qexp_ref, sacc_ref, sc_ref = rest[2 * n:]
    s = pl.program_id(0)
    qi = qi_tab[s]
    u = u_tab[s]
    role = role_tab[s]
    start, pair, finish = _prompt_stages(qi, q_ref, qfill_ref, kfill_ref, k_ref, v_ref, trip_ref,
                                         acc_ref, c_ref, z_ref, p_ref)
    begin, pages, end = _sample_stages(qs_ref, kn_ref, vn_ref, brow_ref, tris_ref, k_refs, v_refs,
                                       qexp_ref, sacc_ref, sc_ref)

    @pl.when(role == SEQ_FIRST)
    def _():
        begin()

    @pl.when(u == 0)
    def _():
        start()
        pair(0, True, True)
        pages()

    @pl.when(jnp.logical_and(u > 0, role != SEQ_NONE))
    def _():
        pair(u, False, False)
        pages()

    @pl.when(jnp.logical_and(u > 0, role == SEQ_NONE))
    def _():
        pair(u, False, False)

    @pl.when(u == qi)
    def _():
        finish(op_ref)

    @pl.when(role == SEQ_LAST)
    def _():
        end(os_ref)


def _attention(bias, page_table, q_p, k_p, v_p, q_s, k_new, v_new, cache_k, cache_v):
    l = q_p.shape[0]
    db, lq, _ = q_s.shape
    n_pages = page_table.shape[1]
    n = PAGES_PER_STEP
    tq, tk = ATT_TQ, ATT_TK
    nq = l // tq
    assert tq == 2 * tk and l % tq == 0 and n_pages % n == 0
    per_seq = n_pages // n
    page_steps = db * per_seq
    steps = [(hp, qi, u) for hp in range(HEAD_PAIRS) for qi in range(nq) for u in range(qi + 1)]
    assert len(steps) >= page_steps, "the sample page stream needs at least one grid step per page step"
    assert per_seq >= 2
    hp_tab, qi_tab, u_tab = (jnp.asarray([t[i] for t in steps], jnp.int32) for i in range(3))
    step = np.arange(len(steps))
    ps = np.minimum(step, page_steps - 1)
    within = ps % per_seq
    role = np.where(step >= page_steps, SEQ_NONE,
                    np.where(within == 0, SEQ_FIRST, np.where(within == per_seq - 1, SEQ_LAST, SEQ_MIDDLE)))
    cols = n_pages - 1 - (within[:, None] * n + np.arange(n)[None, :])
    seq_tab = jnp.asarray(ps // per_seq, jnp.int32)
    role_tab = jnp.asarray(role, jnp.int32)
    page_tab = page_table[(ps // per_seq)[:, None], cols].reshape(-1).astype(jnp.int32)
    qfill, kfill = _bias_lanes(bias)
    qfill = jnp.broadcast_to(qfill[:, :, None, :], (HEAD_PAIRS, 2, tq, LANES))
    kfill = jnp.broadcast_to(kfill[:, None, :], (2, tk, LANES))
    rows = ATT_HEADS * lq
    pool = cache_k.shape[0]
    ck = jnp.transpose(cache_k, (0, 2, 3, 1)).reshape(pool, ATT_WIDTH, PAGE_SIZE)
    cv = jnp.transpose(cache_v, (0, 2, 3, 1)).reshape(pool, ATT_WIDTH, PAGE_SIZE)
    brow = jnp.broadcast_to(jnp.repeat(bias.astype(F32), lq)[:, None], (rows, LANES))

    const = lambda shape: pl.BlockSpec(shape, lambda s, *tabs: (0,) * len(shape))
    tile = pl.BlockSpec((tq, LANES), lambda s, hp, qi, *tabs: (qi[s], hp[s]))
    whole = pl.BlockSpec((l, LANES), lambda s, hp, *tabs: (0, hp[s]))
    seq = pl.BlockSpec((None, lq, ATT_WIDTH), lambda s, hp, qi, u, sq, *tabs: (sq[s], 0, 0))

    def page(i):
        return pl.BlockSpec((None, ATT_WIDTH, PAGE_SIZE), lambda s, hp, qi, u, sq, ro, pg: (pg[s * n + i], 0, 0))

    grid_spec = pltpu.PrefetchScalarGridSpec(
        num_scalar_prefetch=6,
        grid=(len(steps),),
        in_specs=[tile,
                  pl.BlockSpec((None, 2, tq, LANES), lambda s, hp, *tabs: (hp[s], 0, 0, 0)),
                  const((2, tk, LANES)), whole, whole, const((tk, tk)),
                  seq, seq, seq, const((rows, LANES)), const((PAGE_SIZE, PAGE_SIZE + LANES))]
                 + [page(i) for i in range(n)] * 2,
        out_specs=(tile, seq),
        scratch_shapes=[pltpu.VMEM((tq, LANES), F32), pltpu.VMEM((2, tq, LANES), F32),
                        pltpu.VMEM((2, 2, tq, tk), F32), pltpu.VMEM((2, tq, 2 * tk), BF16),
                        pltpu.VMEM((rows, ATT_WIDTH), BF16), pltpu.VMEM((rows, ATT_WIDTH), F32),
                        pltpu.VMEM((rows, LANES), F32)],
    )
    return pl.pallas_call(
        _attn_kernel,
        grid_spec=grid_spec,
        out_shape=(jax.ShapeDtypeStruct((l, ATT_WIDTH), BF16), jax.ShapeDtypeStruct((db, lq, ATT_WIDTH), F32)),
        compiler_params=_params("arbitrary"),
        name="attention",
    )(hp_tab, qi_tab, u_tab, seq_tab, role_tab, page_tab, q_p, qfill, kfill, k_p, v_p, _tri(tk, False),
      q_s, k_new, v_new, brow, _tri(PAGE_SIZE, True), *([ck] * n), *([cv] * n))


def _ssd_kernel(xbc_ref, z_ref, dtr_ref, s0_ref, c0_ref, wconv_ref, bconv_ref, dtb_ref, alog_ref,
                dskip_ref, gn_ref, tril_ref, e_ref,
                y_ref, sn_ref, cn_ref, cbuf_ref, state_ref):
    c = pl.program_id(1)
    cs = xbc_ref.shape[0]
    ck = SSD_CHUNK
    tail = 8
    hist = SSD_CONV - 1

    @pl.when(c == 0)
    def _():
        state_ref[...] = s0_ref[...]
        cbuf_ref[0:tail, :] = jnp.zeros((tail, CONV_DIM), F32)
        cbuf_ref[tail - hist:tail, :] = c0_ref[...]

    cbuf_ref[tail:tail + cs, :] = xbc_ref[...]
    w = wconv_ref[...]
    xall = cbuf_ref[...]
    xc = bconv_ref[...] + pltpu.roll(xall, hist, 0)[tail:] * w[0:1]
    for j in range(1, SSD_CONV):
        tap = xall if j == hist else pltpu.roll(xall, hist - j, 0)
        xc = xc + tap[tail:] * w[j:j + 1]
    cn_ref[...] = cbuf_ref[tail + cs - hist:tail + cs, :]
    last = cbuf_ref[cs:cs + tail, :]
    cbuf_ref[0:tail, :] = last
    xc = _silu(xc)
    dt = _softplus(dtr_ref[...] + dtb_ref[...])
    zg = z_ref[...]
    if cs < ck:
        xc = jnp.concatenate([xc, jnp.zeros((ck - cs, CONV_DIM), F32)], axis=0)
        dt = jnp.concatenate([dt, jnp.zeros((ck - cs, LANES), F32)], axis=0)
        zg = jnp.concatenate([zg, jnp.zeros((ck - cs, SSD_WIDTH), F32)], axis=0)

    a = -jnp.exp(alog_ref[...])
    acum = _dot_exact_rhs(tril_ref[...], dt * a)
    acum_row = acum.T
    e = e_ref[...]
    dt_x = _dot_exact_lhs(dt, e)
    ac_x = _dot_exact_lhs(acum, e)
    xs = xc[:, :SSD_WIDTH]
    xd = xs * dt_x
    xdw = (xd * jnp.exp(ac_x[ck - 1:ck, :] - ac_x)).astype(BF16)
    eac_x = jnp.exp(ac_x)
    row_i = lax.broadcasted_iota(jnp.int32, (ck, ck), 0)
    col_j = lax.broadcasted_iota(jnp.int32, (ck, ck), 1)
    causal = row_i >= col_j
    low = lax.broadcasted_iota(jnp.int32, (1, LANES), 1) < SSD_HEAD_DIM
    per_group = SSD_HEADS // SSD_GROUPS
    ys = []
    gmat = {}
    for hp in range(SSD_WIDTH // LANES):
        g = (2 * hp) // per_group
        b_g = xc[:, SSD_WIDTH + g * SSD_STATE:SSD_WIDTH + (g + 1) * SSD_STATE].astype(BF16)
        c_off = SSD_WIDTH + SSD_GROUPS * SSD_STATE
        c_g = xc[:, c_off + g * SSD_STATE:c_off + (g + 1) * SSD_STATE].astype(BF16)
        if g not in gmat:
            gmat[g] = _dot_nt(c_g, b_g)
        sl = slice(hp * LANES, (hp + 1) * LANES)
        xd_p = xd[:, sl]
        y_p = None
        decs = []
        for k in range(2):
            h = 2 * hp + k
            diff = acum[:, h:h + 1] - acum_row[h:h + 1, :]
            decay = jnp.where(causal, jnp.exp(jnp.minimum(diff, 0.0)), 0.0)
            sc = (gmat[g] * decay).astype(BF16)
            xd_k = jnp.where(low if k == 0 else jnp.logical_not(low), xd_p, 0.0).astype(BF16)
            d = _dot(sc, xd_k)
            y_p = d if y_p is None else y_p + d
            decs.append(jnp.broadcast_to(jnp.exp(acum[ck - 1:ck, h:h + 1]), (SSD_HEAD_DIM, SSD_STATE)))
        s_p = state_ref[hp]
        y_p = y_p + _dot_nt(c_g, s_p.astype(BF16)) * eac_x[:, sl]
        state_ref[hp] = s_p * jnp.concatenate(decs, axis=0) + _dot_tn(xdw[:, sl], b_g)
        ys.append(y_p + dskip_ref[:, sl] * xs[:, sl])
    y = jnp.concatenate(ys, axis=1) * _silu(zg)
    y = _rms(y) * gn_ref[...]
    y_ref[...] = y[:cs, :]

    @pl.when(c == pl.num_programs(1) - 1)
    def _():
        sn_ref[...] = state_ref[...]


def _ssd(xbc, z, dtr, s0, c0, wconv, bconv, dtb, alog, dskip, gn, batch):
    rows = xbc.shape[0]
    l = rows // batch
    cs = min(SSD_CHUNK, l)
    nc = l // cs
    pairs = SSD_WIDTH // LANES
    s0 = s0.reshape(batch, pairs, 2 * SSD_HEAD_DIM, SSD_STATE)
    i = lax.broadcasted_iota(jnp.int32, (SSD_CHUNK, SSD_CHUNK), 0)
    j = lax.broadcasted_iota(jnp.int32, (SSD_CHUNK, SSD_CHUNK), 1)
    tril = (j <= i).astype(BF16)
    eh = lax.broadcasted_iota(jnp.int32, (LANES, SSD_WIDTH), 0)
    ec = lax.broadcasted_iota(jnp.int32, (LANES, SSD_WIDTH), 1) // SSD_HEAD_DIM
    expand = (eh == ec).astype(BF16)
    row = lambda n: pl.BlockSpec((cs, n), lambda b, c: (b * nc + c, 0))
    vec = lambda a: pl.BlockSpec(a.shape, lambda b, c: (0, 0))
    st = pl.BlockSpec((None, pairs, 2 * SSD_HEAD_DIM, SSD_STATE), lambda b, c: (b, 0, 0, 0))
    cv = pl.BlockSpec((None, SSD_CONV - 1, CONV_DIM), lambda b, c: (b, 0, 0))
    y, sn, cn = pl.pallas_call(
        _ssd_kernel,
        grid=(batch, nc),
        in_specs=[row(CONV_DIM), row(SSD_WIDTH), row(LANES), st, cv,
                  vec(wconv), vec(bconv), vec(dtb), vec(alog), vec(dskip), vec(gn), vec(tril), vec(expand)],
        out_specs=(row(SSD_WIDTH), st, cv),
        out_shape=(jax.ShapeDtypeStruct((rows, SSD_WIDTH), F32),
                   jax.ShapeDtypeStruct(s0.shape, F32),
                   jax.ShapeDtypeStruct((batch, SSD_CONV - 1, CONV_DIM), F32)),
        scratch_shapes=[pltpu.VMEM((8 + cs, CONV_DIM), F32),
                        pltpu.VMEM((pairs, 2 * SSD_HEAD_DIM, SSD_STATE), F32)],
        compiler_params=_params("parallel", "arbitrary"),
        name="ssd",
    )(xbc, z, dtr, s0, c0, wconv, bconv, dtb, alog, dskip, gn, tril, expand)
    return y, sn.reshape(batch, SSD_HEADS, SSD_HEAD_DIM, SSD_STATE), cn


def _out_proj_kernel(o_ref, y_ref, x_ref, woa_ref, wob_ref, gpost_ref, gpre_ref, h_ref, hn_ref):
    mix = _dot(o_ref[...].astype(BF16), woa_ref[...]) + _dot(y_ref[...].astype(BF16), wob_ref[...])
    h = x_ref[...] + _rms(mix) * gpost_ref[...]
    h_ref[...] = h
    hn_ref[...] = (_rms(h) * gpre_ref[...]).astype(BF16)


def _out_proj(o, y, x, woa, wob, gpost, gpre):
    rows, d = x.shape
    tm = min(ROW_TILE, rows)
    row = lambda n: pl.BlockSpec((tm, n), lambda i: (i, 0))
    return pl.pallas_call(
        _out_proj_kernel,
        grid=(rows // tm,),
        in_specs=[row(ATT_WIDTH), row(SSD_WIDTH), row(d),
                  _const_spec(woa.shape), _const_spec(wob.shape), _const_spec(gpost.shape), _const_spec(gpre.shape)],
        out_specs=(row(d), row(d)),
        out_shape=(jax.ShapeDtypeStruct((rows, d), F32), jax.ShapeDtypeStruct((rows, d), BF16)),
        compiler_params=_params("parallel"),
        name="out_proj",
    )(o, y, x, woa, wob, gpost, gpre)


def _gelu_tanh(x):
    return 0.5 * x * (1.0 + jnp.tanh(0.7978845608028654 * (x + 0.044715 * (x * x * x))))


def _ffn_kernel(*refs, seq_len):
    if seq_len is None:
        hn_ref, halo_ref, wg_ref, wv_ref, cwg_ref, cwv_ref, cbg_ref, cbv_ref, wd_ref, f_ref, ug_ref, uv_ref, xe_ref = refs
    else:
        hn_ref, stg_ref, stv_ref, wg_ref, wv_ref, cwg_ref, cwv_ref, cbg_ref, cbv_ref, wd_ref, f_ref, ug_ref, uv_ref = refs
    i = pl.program_id(0)
    j = pl.program_id(1)
    tm = hn_ref.shape[0]

    @pl.when(j == 0)
    def _():
        f_ref[...] = jnp.zeros_like(f_ref)

    if seq_len is None:
        @pl.when(j == 0)
        def _():
            halo = halo_ref[...]
            xe_ref[0:FFN_HALO, :] = jnp.where(i > 0, halo, jnp.zeros_like(halo))
            xe_ref[FFN_HALO:, :] = hn_ref[...]

        def conv(w_ref, cw_ref, cb_ref, u_ref):
            up = _dot(xe_ref[...], w_ref[...])
            cw = cw_ref[...]
            y = (cb_ref[...] + pltpu.roll(up, 2, 0)[FFN_HALO:] * cw[0:1]
                 + pltpu.roll(up, 1, 0)[FFN_HALO:] * cw[1:2] + up[FFN_HALO:] * cw[2:3])
            u_ref[...] = up[FFN_HALO + tm - 8:]
            return y
        gate = conv(wg_ref, cwg_ref, cbg_ref, ug_ref)
        val = conv(wv_ref, cwv_ref, cbv_ref, uv_ref)
    else:
        t = lax.broadcasted_iota(jnp.int32, (tm, 1), 0) % seq_len

        def conv(w_ref, cw_ref, cb_ref, st_ref, u_ref):
            up = _dot(hn_ref[...], w_ref[...])
            st = st_ref[...]
            cw = cw_ref[...]
            prev2 = jnp.where(t < 2, st, pltpu.roll(up, 2, 0))
            prev1 = jnp.where(t < 1, pltpu.roll(st, tm - 1, 0), pltpu.roll(up, 1, 0))
            u_ref[...] = up
            return cb_ref[...] + prev2 * cw[0:1] + prev1 * cw[1:2] + up * cw[2:3]
        gate = conv(wg_ref, cwg_ref, cbg_ref, stg_ref, ug_ref)
        val = conv(wv_ref, cwv_ref, cbv_ref, stv_ref, uv_ref)

    f_ref[...] += _dot((_gelu_tanh(gate) * val).astype(BF16), wd_ref[...])


def _ffn(hn, w_up, cw, cb, w_down, state=None, seq_len=None):
    rows, d = hn.shape
    dff = w_down.shape[0]
    tf = FFN_TF
    nj = dff // tf
    tm = min(FFN_TM, rows)
    gcol = lambda r: pl.BlockSpec((r, tf), lambda i, j: (0, j))
    vcol = lambda r: pl.BlockSpec((r, tf), lambda i, j: (0, j + nj))
    hn_spec = pl.BlockSpec((tm, d), lambda i, j: (i, 0))
    common = [pl.BlockSpec((d, tf), lambda i, j: (0, j)), pl.BlockSpec((d, tf), lambda i, j: (0, j + nj)),
              gcol(FFN_CONV), vcol(FFN_CONV), gcol(1), vcol(1),
              pl.BlockSpec((tf, d), lambda i, j: (j, 0))]
    common_args = (w_up, w_up, cw, cw, cb, cb, w_down)
    f_spec = pl.BlockSpec((tm, d), lambda i, j: (i, 0))
    if seq_len is None:
        per = tm // FFN_HALO
        halo = pl.BlockSpec((FFN_HALO, d), lambda i, j: (jnp.maximum(i * per - 1, 0), 0))
        in_specs = [hn_spec, halo] + common
        args = (hn, hn) + common_args
        u_rows = 8 * (rows // tm)
        u_spec = pl.BlockSpec((8, tf), lambda i, j: (i, j))
        scratch = [pltpu.VMEM((FFN_HALO + tm, d), BF16)]
    else:
        assert rows == tm
        in_specs = [hn_spec, pl.BlockSpec((tm, tf), lambda i, j: (0, j)),
                    pl.BlockSpec((tm, tf), lambda i, j: (0, j + nj))] + common
        args = (hn, state, state) + common_args
        u_rows = tm
        u_spec = pl.BlockSpec((u_rows, tf), lambda i, j: (0, j))
        scratch = []
    return pl.pallas_call(
        functools.partial(_ffn_kernel, seq_len=seq_len),
        grid=(rows // tm, nj),
        in_specs=in_specs,
        out_specs=(f_spec, u_spec, u_spec),
        out_shape=(jax.ShapeDtypeStruct((rows, d), F32),
                   jax.ShapeDtypeStruct((u_rows, dff), F32),
                   jax.ShapeDtypeStruct((u_rows, dff), F32)),
        scratch_shapes=scratch,
        compiler_params=_params("arbitrary", "arbitrary"),
        name="ffn",
    )(*args)


def _ple_kernel(h_ref, f_ref, p_ref, gpost_ref, wg_ref, wp_ref, o_ref):
    h = h_ref[...] + _rms(f_ref[...]) * gpost_ref[...]
    gate = 1.0 / (1.0 + jnp.exp(-_dot(h.astype(BF16), wg_ref[...])))
    o_ref[...] = h + _dot(p_ref[...].astype(BF16), wp_ref[...]) * gate


def _ple(h, f, p, gpost, wg, wp):
    rows, d = h.shape
    tm = min(ROW_TILE, rows)
    row = lambda n: pl.BlockSpec((tm, n), lambda i: (i, 0))
    return pl.pallas_call(
        _ple_kernel,
        grid=(rows // tm,),
        in_specs=[row(d), row(d), row(p.shape[1]),
                  _const_spec(gpost.shape), _const_spec(wg.shape), _const_spec(wp.shape)],
        out_specs=row(d),
        out_shape=jax.ShapeDtypeStruct((rows, d), F32),
        compiler_params=_params("parallel"),
        name="ple",
    )(h, f, p, gpost, wg, wp)


def _after_attention(x, p, lw, batch, o, kf, vf, z, xbc, dtr, ssm0, conv0, ffn0):
    rows, d = x.shape
    l = rows // batch
    sample = ffn0 is not None
    y, ssm_new, conv_new = _ssd(xbc, z, dtr, ssm0, conv0, lw["w_conv"], lw["b_conv"], lw["dt_bias"], lw["a_log"],
                                lw["d_skip"], lw["g_ssd_norm"], batch)
    h1, hn = _out_proj(o, y, x, lw["w_out_a"], lw["w_out_b"], lw["g_post_mix"], lw["g_pre_ffn"])
    dff2 = lw["w_up"].shape[1]
    if sample:
        st = jnp.pad(ffn0, ((0, 0), (0, l - (FFN_CONV - 1)), (0, 0))).reshape(rows, dff2)
        f, ug, uv = _ffn(hn, lw["w_up"], lw["w_ffn_conv"], lw["b_ffn_conv"], lw["w_down"], state=st, seq_len=l)
        up = jnp.concatenate([ug, uv], axis=1).reshape(batch, l, dff2)
        ffn_new = up[:, l - (FFN_CONV - 1):, :]
    else:
        f, ug, uv = _ffn(hn, lw["w_up"], lw["w_ffn_conv"], lw["b_ffn_conv"], lw["w_down"])
        ffn_new = jnp.concatenate([ug, uv], axis=1)[None, -(FFN_CONV - 1):, :]
    out = _ple(h1, f, p, lw["g_post_ffn"], lw["w_ple_gate"], lw["w_ple_proj"])
    return out, (kf, vf, ssm_new, conv_new, ffn_new)


def _prep_weights(i, g_pre_mix, w_in, sb_bias, w_conv, b_conv, dt_bias, a_log, d_skip, g_ssd_norm, w_out,
                  g_post_mix, g_pre_ffn, w_up, w_ffn_conv, b_ffn_conv, w_down, g_post_ffn, w_ple_gate, w_ple_proj):
    row = lambda a: a[i][None, :].astype(F32)
    lane_pad = lambda a: jnp.pad(a[i].astype(F32), (0, LANES - a.shape[1]))[None, :]
    w = w_in[i].astype(BF16)
    o_dt = 3 * ATT_WIDTH + SSD_WIDTH + CONV_DIM
    return {
        "g_pre_mix": row(g_pre_mix),
        "w_in": w,
        "w_dt": jnp.pad(w[:, o_dt:], ((0, 0), (0, LANES - SSD_HEADS))),
        "sb_bias": sb_bias[i].astype(F32),
        "w_conv": w_conv[i].astype(F32), "b_conv": row(b_conv),
        "dt_bias": lane_pad(dt_bias), "a_log": lane_pad(a_log),
        "d_skip": jnp.repeat(d_skip[i].astype(F32), SSD_HEAD_DIM)[None, :],
        "g_ssd_norm": row(g_ssd_norm),
        "w_out_a": w_out[i, :ATT_WIDTH].astype(BF16), "w_out_b": w_out[i, ATT_WIDTH:].astype(BF16),
        "g_post_mix": row(g_post_mix), "g_pre_ffn": row(g_pre_ffn),
        "w_up": w_up[i].astype(BF16), "w_ffn_conv": w_ffn_conv[i].astype(F32), "b_ffn_conv": row(b_ffn_conv),
        "w_down": w_down[i].astype(BF16), "g_post_ffn": row(g_post_ffn),
        "w_ple_gate": w_ple_gate[i].astype(BF16), "w_ple_proj": w_ple_proj[i].astype(BF16),
    }


def kernel(x_prompt, x_sample, cache_k, cache_v, state_ssm, state_conv, state_ffn_conv, page_table, p_prompt, p_sample, g_pre_mix, w_in, sb_bias, w_conv, b_conv, dt_bias, a_log, d_skip, g_ssd_norm, w_out, g_post_mix, g_pre_ffn, w_up, w_ffn_conv, b_ffn_conv, w_down, g_post_ffn, w_ple_gate, w_ple_proj):
    bp, lp, d = x_prompt.shape
    db, ls, _ = x_sample.shape
    depth = w_in.shape[0]
    dff2 = w_up.shape[2]
    hp = x_prompt.reshape(bp * lp, d)
    hs = x_sample.reshape(db * ls, d)
    outs_p, outs_s = [], []
    for i in range(depth):
        lw = _prep_weights(i, g_pre_mix, w_in, sb_bias, w_conv, b_conv, dt_bias, a_log, d_skip, g_ssd_norm, w_out,
                           g_post_mix, g_pre_ffn, w_up, w_ffn_conv, b_ffn_conv, w_down, g_post_ffn,
                           w_ple_gate, w_ple_proj)
        ssm0 = jnp.zeros((bp, SSD_HEADS, SSD_HEAD_DIM, SSD_STATE), F32)
        conv0 = jnp.zeros((bp, SSD_CONV - 1, CONV_DIM), F32)
        proj = lambda x, q_dtype: _in_proj(x, lw["g_pre_mix"], lw["w_in"], lw["w_dt"], q_dtype)
        q_p, kf_p, vf_p, kb_p, vb_p, z_p, xbc_p, dt_p = proj(hp, BF16)
        q_s, kf_s, vf_s, _, _, z_s, xbc_s, dt_s = proj(hs, F32)
        seq = lambda a: a.reshape(db, ls, ATT_WIDTH)
        o_p, o_s = _attention(lw["sb_bias"], page_table, q_p, kb_p, vb_p, seq(q_s), seq(kf_s), seq(vf_s),
                              cache_k[i], cache_v[i])
        hp, st_p = _after_attention(hp, p_prompt[i].reshape(bp * lp, -1), lw, bp, o_p, kf_p, vf_p, z_p, xbc_p, dt_p,
                                    ssm0, conv0, None)
        hs, st_s = _after_attention(hs, p_sample[i].reshape(db * ls, -1), lw, db, o_s.reshape(db * ls, ATT_WIDTH),
                                    kf_s, vf_s, z_s, xbc_s, dt_s, state_ssm[i], state_conv[i], state_ffn_conv[i])
        outs_p.append(st_p)
        outs_s.append(st_s)

    def stack(outs, j, shape):
        return jnp.stack([o[j].reshape(shape) for o in outs], axis=0)

    kv_p = (bp, lp, ATT_HEADS, ATT_HEAD_DIM)
    kv_s = (db, ls, ATT_HEADS, ATT_HEAD_DIM)
    ssm_p = (bp, SSD_HEADS, SSD_HEAD_DIM, SSD_STATE)
    ssm_s = (db, SSD_HEADS, SSD_HEAD_DIM, SSD_STATE)
    return (hp.reshape(bp, lp, d), hs.reshape(db, ls, d),
            stack(outs_p, 0, kv_p), stack(outs_s, 0, kv_s), stack(outs_p, 1, kv_p), stack(outs_s, 1, kv_s),
            stack(outs_p, 2, ssm_p), stack(outs_s, 2, ssm_s),
            stack(outs_p, 3, (bp, SSD_CONV - 1, CONV_DIM)), stack(outs_s, 3, (db, SSD_CONV - 1, CONV_DIM)),
            stack(outs_p, 4, (bp, FFN_CONV - 1, dff2)), stack(outs_s, 4, (db, FFN_CONV - 1, dff2)))
```

```python
import functools

import jax
import jax.numpy as jnp
import numpy as np
from jax import lax
from jax.experimental import pallas as pl
from jax.experimental.pallas import tpu as pltpu

F32 = jnp.float32
BF16 = jnp.bfloat16

ATT_HEADS = 16
ATT_HEAD_DIM = 64
ATT_WIDTH = ATT_HEADS * ATT_HEAD_DIM
SSD_HEADS = 16
SSD_HEAD_DIM = 64
SSD_WIDTH = SSD_HEADS * SSD_HEAD_DIM
SSD_GROUPS = 2
SSD_STATE = 128
SSD_CONV = 4
SSD_CHUNK = 128
CONV_DIM = SSD_WIDTH + 2 * SSD_GROUPS * SSD_STATE
FFN_CONV = 3
RMS_EPS = 1e-6
PAGE_SIZE = 128

LANES = 128
HEAD_PAIRS = ATT_WIDTH // LANES
VMEM_LIMIT = 56 * 1024 * 1024

ROW_TILE = 512
IN_PROJ_TILE = 256
ATT_TQ = 512
ATT_TK = 256
FFN_TM = 1024
FFN_TF = 512
FFN_HALO = 16
PAGES_PER_STEP = 4


def _params(*sem):
    return pltpu.CompilerParams(dimension_semantics=sem, vmem_limit_bytes=VMEM_LIMIT)


def _rms(x):
    return x * lax.rsqrt(jnp.mean(x * x, axis=-1, keepdims=True) + RMS_EPS)


def _softplus(x):
    return jnp.maximum(x, 0.0) + jnp.log(1.0 + jnp.exp(-jnp.abs(x)))


def _silu(x):
    return x * (1.0 / (1.0 + jnp.exp(-x)))


def _dot(a, b):
    return jnp.dot(a, b, preferred_element_type=F32)


def _dot_nt(a, b):
    return lax.dot_general(a, b, (((1,), (1,)), ((), ())), preferred_element_type=F32)


def _dot_tn(a, b):
    return lax.dot_general(a, b, (((0,), (0,)), ((), ())), preferred_element_type=F32)


def _split3(x):
    hi = x.astype(BF16)
    r1 = x - hi.astype(F32)
    mid = r1.astype(BF16)
    lo = (r1 - mid.astype(F32)).astype(BF16)
    return hi, mid, lo


def _dot_exact_rhs(m, x):
    hi, mid, lo = _split3(x)
    return _dot(m, hi) + _dot(m, mid) + _dot(m, lo)


def _dot_exact_lhs(x, m):
    hi, mid, lo = _split3(x)
    return _dot(hi, m) + _dot(mid, m) + _dot(lo, m)


def _const_spec(shape):
    return pl.BlockSpec(shape, lambda *_: (0,) * len(shape))


def _in_proj_kernel(x_ref, g_ref, w_ref, wdt_ref,
                    q_ref, kf_ref, vf_ref, kb_ref, vb_ref, z_ref, xbc_ref, dt_ref):
    o_z = 3 * ATT_WIDTH
    o_x = o_z + SSD_WIDTH
    u = (_rms(x_ref[...]) * g_ref[...]).astype(BF16)
    qkv = _dot(u, w_ref[:, :o_z])
    q_ref[...] = (qkv[:, :ATT_WIDTH] * (ATT_HEAD_DIM ** -0.5)).astype(q_ref.dtype)
    k = qkv[:, ATT_WIDTH:2 * ATT_WIDTH]
    v = qkv[:, 2 * ATT_WIDTH:]
    kf_ref[...] = k
    vf_ref[...] = v
    kb_ref[...] = k.astype(BF16)
    vb_ref[...] = v.astype(BF16)
    z_ref[...] = _dot(u, w_ref[:, o_z:o_x])
    xbc_ref[...] = _dot(u, w_ref[:, o_x:o_x + CONV_DIM])
    dt_ref[...] = _dot(u, wdt_ref[...])


def _in_proj(x, g, w, wdt, q_dtype):
    rows, d = x.shape
    tm = min(IN_PROJ_TILE, rows)
    row = lambda n: pl.BlockSpec((tm, n), lambda i: (i, 0))
    res = lambda a: pl.BlockSpec(a.shape, lambda i: (0, 0), pipeline_mode=pl.Buffered(1))
    out_shape = (
        jax.ShapeDtypeStruct((rows, ATT_WIDTH), q_dtype),
        jax.ShapeDtypeStruct((rows, ATT_WIDTH), F32),
        jax.ShapeDtypeStruct((rows, ATT_WIDTH), F32),
        jax.ShapeDtypeStruct((rows, ATT_WIDTH), BF16),
        jax.ShapeDtypeStruct((rows, ATT_WIDTH), BF16),
        jax.ShapeDtypeStruct((rows, SSD_WIDTH), F32),
        jax.ShapeDtypeStruct((rows, CONV_DIM), F32),
        jax.ShapeDtypeStruct((rows, LANES), F32),
    )
    return pl.pallas_call(
        _in_proj_kernel,
        grid=(rows // tm,),
        in_specs=[row(d), res(g), res(w), res(wdt)],
        out_specs=tuple(row(s.shape[1]) for s in out_shape),
        out_shape=out_shape,
        compiler_params=_params("parallel"),
        name="in_proj",
    )(x, g, w, wdt)


def _stick_block(z, carry, tri, mask):
    rows, tk = z.shape
    sp = _softplus(z)
    if mask is not None:
        sp = jnp.where(mask, sp, 0.0)
    cs = _dot(sp.astype(BF16), tri)
    wide = carry if tk == LANES else jnp.concatenate([carry] * (tk // LANES), axis=1)
    p = jnp.exp(z - cs[:, :tk] - wide)
    if mask is not None:
        p = jnp.where(mask, p, 0.0)
    total = cs[:, tk:] if tri.shape[1] > tk else jnp.broadcast_to(cs[:, 0:1], (rows, LANES))
    return p, carry + total


def _tri(tk, with_ones):
    n = tk + LANES if with_ones else tk
    s = lax.broadcasted_iota(jnp.int32, (tk, n), 0)
    j = lax.broadcasted_iota(jnp.int32, (tk, n), 1)
    return ((s >= j) | (j >= tk)).astype(BF16)


def _prompt_stages(qi, q_ref, qfill_ref, kfill_ref, k_ref, v_ref, tri_ref, acc_ref, c_ref, z_ref, p_ref):
    tq, tk = ATT_TQ, ATT_TK
    n_blocks = (qi + 1) * (tq // tk)
    low = lax.broadcasted_iota(jnp.int32, (1, LANES), 1) < ATT_HEAD_DIM
    own = (low, jnp.logical_not(low))
    q = q_ref[...]
    qs = tuple(jnp.where(own[a], q, qfill_ref[a]) for a in range(2))
    tri = tri_ref[...]
    q_pos = qi * tq + lax.broadcasted_iota(jnp.int32, (tq, tk), 0)
    k_off = lax.broadcasted_iota(jnp.int32, (tq, tk), 1)

    def rows(n):
        return pl.ds(pl.multiple_of(jnp.maximum(n_blocks - 1 - n, 0) * tk, tk), tk)

    def scores(n, slot):
        kb = k_ref[rows(n), :]
        for a in range(2):
            z_ref[slot, a] = _dot_nt(qs[a], jnp.where(own[a], kb, kfill_ref[a]))

    def weights(n, slot, masked):
        mask = ((n_blocks - 1 - n) * tk + k_off) < q_pos if masked else None
        for a in range(2):
            p, c_ref[a] = _stick_block(z_ref[slot, a], c_ref[a], tri, mask)
            p_ref[slot, :, a * tk:(a + 1) * tk] = p.astype(BF16)

    def apply(n, slot):
        vb = v_ref[rows(n), :]
        vz = jnp.zeros_like(vb)
        vcat = jnp.concatenate([jnp.where(low, vb, vz), jnp.where(low, vz, vb)], axis=0)
        acc_ref[...] += _dot(p_ref[slot], vcat)

    def pair(u, masked, first):
        n = 2 * u
        scores(n + 1, 1)
        if not first:
            apply(n - 1, 1)
        weights(n, 0, masked)
        scores(n + 2, 0)
        apply(n, 0)
        weights(n + 1, 1, masked)

    def start():
        acc_ref[...] = jnp.zeros_like(acc_ref)
        c_ref[...] = jnp.zeros_like(c_ref)
        scores(0, 0)

    def finish(o_ref):
        apply(n_blocks - 1, 1)
        o_ref[...] = acc_ref[...].astype(o_ref.dtype)

    return start, pair, finish


def _bias_lanes(bias):
    b = bias.astype(F32)
    b0 = b.astype(BF16)
    r = b - b0.astype(F32)
    b1 = r.astype(BF16)
    b2 = (r - b1.astype(F32)).astype(BF16)
    terms = jnp.stack([b0, b1, b2], axis=-1).reshape(HEAD_PAIRS, 2, 3)
    pad = jnp.zeros((HEAD_PAIRS, ATT_HEAD_DIM - 3), BF16)
    zero = jnp.zeros((HEAD_PAIRS, ATT_HEAD_DIM), BF16)
    first = jnp.concatenate([zero, terms[:, 0], pad], axis=1)
    second = jnp.concatenate([terms[:, 1], pad, zero], axis=1)
    qfill = jnp.stack([first, second], axis=1)
    ones = jnp.concatenate([jnp.ones((3,), BF16), jnp.zeros((ATT_HEAD_DIM - 3,), BF16)])
    zeros = jnp.zeros((ATT_HEAD_DIM,), BF16)
    kfill = jnp.stack([jnp.concatenate([zeros, ones]), jnp.concatenate([ones, zeros])])
    return qfill, kfill


def _sample_stages(q_ref, kn_ref, vn_ref, brow_ref, tri_ref, k_refs, v_refs, qexp_ref, acc_ref, c_ref,
                   zs_ref, vcat_ref):
    n = len(k_refs)
    rows = ATT_HEADS * q_ref.shape[0]
    lq = q_ref.shape[0]
    col_head = lax.broadcasted_iota(jnp.int32, (1, ATT_WIDTH), 1) // ATT_HEAD_DIM
    tri = tri_ref[...]
    brow = brow_ref[...]

    def queries():
        qt = jnp.concatenate([q_ref[...]] * ATT_HEADS, axis=0)
        row_head = lax.broadcasted_iota(jnp.int32, (rows, 1), 0) // lq
        qexp_ref[...] = jnp.where(row_head == col_head, qt, 0.0).astype(BF16)

    def new_keys():
        pad = jnp.zeros((PAGE_SIZE - lq, ATT_WIDTH), F32)
        kn = jnp.concatenate([kn_ref[...], pad], axis=0)
        vn = jnp.concatenate([vn_ref[...], pad], axis=0)
        t = lax.broadcasted_iota(jnp.int32, (rows, PAGE_SIZE), 0) % lq
        tok = lax.broadcasted_iota(jnp.int32, (rows, PAGE_SIZE), 1)
        z = _dot_nt(qexp_ref[...], kn.astype(BF16)) + brow
        p, c_ref[...] = _stick_block(z, jnp.zeros_like(brow), tri, tok < t)
        acc_ref[...] = _dot(p.astype(BF16), vn.astype(BF16))

    def score(slot):
        kcat = jnp.concatenate([k_refs[i][...].astype(BF16) for i in reversed(range(n))], axis=1)
        vcat_ref[slot] = jnp.concatenate([v_refs[i][...].astype(BF16) for i in reversed(range(n))], axis=1)
        zs_ref[slot] = _dot(qexp_ref[...], kcat)

    def apply(slot):
        z = zs_ref[slot]
        c = c_ref[...]
        ps = [None] * n
        for i in reversed(range(n)):
            zi = z[:, i * PAGE_SIZE:(i + 1) * PAGE_SIZE] + brow
            p, c = _stick_block(zi, c, tri, None)
            ps[i] = p.astype(BF16)
        c_ref[...] = c
        acc_ref[...] += _dot_nt(jnp.concatenate(ps, axis=1), vcat_ref[slot])

    def end(o_ref):
        out = jnp.zeros((lq, ATT_WIDTH), F32)
        for h in range(ATT_HEADS):
            out = jnp.where(col_head == h, acc_ref[h * lq:(h + 1) * lq, :], out)
        o_ref[...] = out

    return queries, new_keys, score, apply, end


SEQ_FIRST, SEQ_MIDDLE, SEQ_LAST, SEQ_NONE = 0, 1, 2, 3


def _attn_kernel(hp_tab, qi_tab, u_tab, seq_tab, oseq_tab, role_tab, prev_tab, page_tab,
                 q_ref, qfill_ref, kfill_ref, k_ref, v_ref, trip_ref,
                 qs_ref, kn_ref, vn_ref, brow_ref, tris_ref, *rest):
    n = PAGES_PER_STEP
    k_refs, v_refs = rest[:n], rest[n:2 * n]
    op_ref, os_ref, acc_ref, c_ref, z_ref, p_ref, qexp_ref, sacc_ref, sc_ref, zs_ref, vcat_ref = rest[2 * n:]
    s = pl.program_id(0)
    qi = qi_tab[s]
    u = u_tab[s]
    role = role_tab[s]
    prev = prev_tab[s]
    odd = lax.rem(s, 2)
    start, pair, finish = _prompt_stages(qi, q_ref, qfill_ref, kfill_ref, k_ref, v_ref, trip_ref,
                                         acc_ref, c_ref, z_ref, p_ref)
    queries, new_keys, score, apply, end = _sample_stages(qs_ref, kn_ref, vn_ref, brow_ref, tris_ref, k_refs, v_refs,
                                                          qexp_ref, sacc_ref, sc_ref, zs_ref, vcat_ref)

    @pl.when(s == 0)
    def _():
        zs_ref[1] = jnp.zeros(zs_ref.shape[1:], zs_ref.dtype)
        vcat_ref[1] = jnp.zeros(vcat_ref.shape[1:], vcat_ref.dtype)
        sc_ref[...] = jnp.zeros_like(sc_ref)
        sacc_ref[...] = jnp.zeros_like(sacc_ref)

    @pl.when(role == SEQ_FIRST)
    def _():
        queries()

    stream = jnp.logical_or(role != SEQ_NONE, prev != SEQ_NONE)
    for parity in range(2):
        mine = odd == parity

        @pl.when(jnp.logical_and(u == 0, mine))
        def _():
            start()
            pair(0, True, True)
            apply(1 - parity)
            score(parity)

        @pl.when(jnp.logical_and(jnp.logical_and(u > 0, stream), mine))
        def _():
            pair(u, False, False)
            apply(1 - parity)
            score(parity)

    @pl.when(jnp.logical_and(u > 0, jnp.logical_not(stream)))
    def _():
        pair(u, False, False)

    @pl.when(u == qi)
    def _():
        finish(op_ref)

    @pl.when(prev == SEQ_LAST)
    def _():
        end(os_ref)

    @pl.when(role == SEQ_FIRST)
    def _():
        new_keys()


def _attention(bias, page_table, q_p, k_p, v_p, q_s, k_new, v_new, cache_k, cache_v):
    l = q_p.shape[0]
    db, lq, _ = q_s.shape
    n_pages = page_table.shape[1]
    n = PAGES_PER_STEP
    tq, tk = ATT_TQ, ATT_TK
    nq = l // tq
    assert tq == 2 * tk and l % tq == 0 and n_pages % n == 0
    per_seq = n_pages // n
    page_steps = db * per_seq
    steps = [(hp, qi, u) for hp in range(HEAD_PAIRS) for qi in range(nq) for u in range(qi + 1)]
    assert len(steps) > page_steps, "the page stream needs a grid step per page step plus one to drain its pipeline"
    assert per_seq >= 2
    hp_tab, qi_tab, u_tab = (jnp.asarray([t[i] for t in steps], jnp.int32) for i in range(3))
    step = np.arange(len(steps))
    ps = np.minimum(step, page_steps - 1)
    within = ps % per_seq
    role = np.where(step >= page_steps, SEQ_NONE,
                    np.where(within == 0, SEQ_FIRST, np.where(within == per_seq - 1, SEQ_LAST, SEQ_MIDDLE)))
    cols = n_pages - 1 - (within[:, None] * n + np.arange(n)[None, :])
    seq_tab = jnp.asarray(ps // per_seq, jnp.int32)
    role_tab = jnp.asarray(role, jnp.int32)
    oseq_tab = jnp.asarray(np.concatenate([[0], ps[:-1] // per_seq]), jnp.int32)
    prev_tab = jnp.asarray(np.concatenate([[SEQ_NONE], role[:-1]]), jnp.int32)
    page_tab = page_table[(ps // per_seq)[:, None], cols].reshape(-1).astype(jnp.int32)
    qfill, kfill = _bias_lanes(bias)
    qfill = jnp.broadcast_to(qfill[:, :, None, :], (HEAD_PAIRS, 2, tq, LANES))
    kfill = jnp.broadcast_to(kfill[:, None, :], (2, tk, LANES))
    rows = ATT_HEADS * lq
    pool = cache_k.shape[0]
    ck = jnp.transpose(cache_k, (0, 2, 3, 1)).reshape(pool, ATT_WIDTH, PAGE_SIZE)
    cv = jnp.transpose(cache_v, (0, 2, 3, 1)).reshape(pool, ATT_WIDTH, PAGE_SIZE)
    brow = jnp.broadcast_to(jnp.repeat(bias.astype(F32), lq)[:, None], (rows, LANES))

    const = lambda shape: pl.BlockSpec(shape, lambda s, *tabs: (0,) * len(shape))
    tile = pl.BlockSpec((tq, LANES), lambda s, hp, qi, *tabs: (qi[s], hp[s]))
    whole = pl.BlockSpec((l, LANES), lambda s, hp, *tabs: (0, hp[s]))
    seq = pl.BlockSpec((None, lq, ATT_WIDTH), lambda s, hp, qi, u, sq, *tabs: (sq[s], 0, 0))
    oseq = pl.BlockSpec((None, lq, ATT_WIDTH), lambda s, hp, qi, u, sq, osq, *tabs: (osq[s], 0, 0))

    def page(i):
        return pl.BlockSpec((None, ATT_WIDTH, PAGE_SIZE), lambda s, *tabs: (tabs[-1][s * n + i], 0, 0))

    grid_spec = pltpu.PrefetchScalarGridSpec(
        num_scalar_prefetch=8,
        grid=(len(steps),),
        in_specs=[tile,
                  pl.BlockSpec((None, 2, tq, LANES), lambda s, hp, *tabs: (hp[s], 0, 0, 0)),
                  const((2, tk, LANES)), whole, whole, const((tk, tk)),
                  seq, seq, seq, const((rows, LANES)), const((PAGE_SIZE, PAGE_SIZE + LANES))]
                 + [page(i) for i in range(n)] * 2,
        out_specs=(tile, oseq),
        scratch_shapes=[pltpu.VMEM((tq, LANES), F32), pltpu.VMEM((2, tq, LANES), F32),
                        pltpu.VMEM((2, 2, tq, tk), F32), pltpu.VMEM((2, tq, 2 * tk), BF16),
                        pltpu.VMEM((rows, ATT_WIDTH), BF16), pltpu.VMEM((rows, ATT_WIDTH), F32),
                        pltpu.VMEM((rows, LANES), F32),
                        pltpu.VMEM((2, rows, n * PAGE_SIZE), F32), pltpu.VMEM((2, ATT_WIDTH, n * PAGE_SIZE), BF16)],
    )
    return pl.pallas_call(
        _attn_kernel,
        grid_spec=grid_spec,
        out_shape=(jax.ShapeDtypeStruct((l, ATT_WIDTH), BF16), jax.ShapeDtypeStruct((db, lq, ATT_WIDTH), F32)),
        compiler_params=_params("arbitrary"),
        name="attention",
    )(hp_tab, qi_tab, u_tab, seq_tab, oseq_tab, role_tab, prev_tab, page_tab, q_p, qfill, kfill, k_p, v_p, _tri(tk, False),
      q_s, k_new, v_new, brow, _tri(PAGE_SIZE, True), *([ck] * n), *([cv] * n))


def _ssd_kernel(xbc_ref, z_ref, dtr_ref, s0_ref, c0_ref, wconv_ref, bconv_ref, dtb_ref, alog_ref,
                dskip_ref, gn_ref, tril_ref, e_ref,
                y_ref, sn_ref, cn_ref, cbuf_ref, state_ref):
    c = pl.program_id(1)
    cs = xbc_ref.shape[0]
    ck = SSD_CHUNK
    tail = 8
    hist = SSD_CONV - 1

    @pl.when(c == 0)
    def _():
        state_ref[...] = s0_ref[...]
        cbuf_ref[0:tail, :] = jnp.zeros((tail, CONV_DIM), F32)
        cbuf_ref[tail - hist:tail, :] = c0_ref[...]

    cbuf_ref[tail:tail + cs, :] = xbc_ref[...]
    w = wconv_ref[...]
    xall = cbuf_ref[...]
    xc = bconv_ref[...] + pltpu.roll(xall, hist, 0)[tail:] * w[0:1]
    for j in range(1, SSD_CONV):
        tap = xall if j == hist else pltpu.roll(xall, hist - j, 0)
        xc = xc + tap[tail:] * w[j:j + 1]
    cn_ref[...] = cbuf_ref[tail + cs - hist:tail + cs, :]
    last = cbuf_ref[cs:cs + tail, :]
    cbuf_ref[0:tail, :] = last
    xc = _silu(xc)
    dt = _softplus(dtr_ref[...] + dtb_ref[...])
    zg = z_ref[...]
    if cs < ck:
        xc = jnp.concatenate([xc, jnp.zeros((ck - cs, CONV_DIM), F32)], axis=0)
        dt = jnp.concatenate([dt, jnp.zeros((ck - cs, LANES), F32)], axis=0)
        zg = jnp.concatenate([zg, jnp.zeros((ck - cs, SSD_WIDTH), F32)], axis=0)

    a = -jnp.exp(alog_ref[...])
    acum = _dot_exact_rhs(tril_ref[...], dt * a)
    acum_row = acum.T
    e = e_ref[...]
    dt_x = _dot_exact_lhs(dt, e)
    ac_x = _dot_exact_lhs(acum, e)
    xs = xc[:, :SSD_WIDTH]
    xd = xs * dt_x
    xdw = (xd * jnp.exp(ac_x[ck - 1:ck, :] - ac_x)).astype(BF16)
    eac_x = jnp.exp(ac_x)
    row_i = lax.broadcasted_iota(jnp.int32, (ck, ck), 0)
    col_j = lax.broadcasted_iota(jnp.int32, (ck, ck), 1)
    causal = row_i >= col_j
    low = lax.broadcasted_iota(jnp.int32, (1, LANES), 1) < SSD_HEAD_DIM
    per_group = SSD_HEADS // SSD_GROUPS
    ys = []
    gmat = {}
    for hp in range(SSD_WIDTH // LANES):
        g = (2 * hp) // per_group
        b_g = xc[:, SSD_WIDTH + g * SSD_STATE:SSD_WIDTH + (g + 1) * SSD_STATE].astype(BF16)
        c_off = SSD_WIDTH + SSD_GROUPS * SSD_STATE
        c_g = xc[:, c_off + g * SSD_STATE:c_off + (g + 1) * SSD_STATE].astype(BF16)
        if g not in gmat:
            gmat[g] = _dot_nt(c_g, b_g)
        sl = slice(hp * LANES, (hp + 1) * LANES)
        xd_p = xd[:, sl]
        y_p = None
        decs = []
        for k in range(2):
            h = 2 * hp + k
            diff = acum[:, h:h + 1] - acum_row[h:h + 1, :]
            decay = jnp.where(causal, jnp.exp(jnp.minimum(diff, 0.0)), 0.0)
            sc = (gmat[g] * decay).astype(BF16)
            xd_k = jnp.where(low if k == 0 else jnp.logical_not(low), xd_p, 0.0).astype(BF16)
            d = _dot(sc, xd_k)
            y_p = d if y_p is None else y_p + d
            decs.append(jnp.broadcast_to(jnp.exp(acum[ck - 1:ck, h:h + 1]), (SSD_HEAD_DIM, SSD_STATE)))
        s_p = state_ref[hp]
        y_p = y_p + _dot_nt(c_g, s_p.astype(BF16)) * eac_x[:, sl]
        state_ref[hp] = s_p * jnp.concatenate(decs, axis=0) + _dot_tn(xdw[:, sl], b_g)
        ys.append(y_p + dskip_ref[:, sl] * xs[:, sl])
    y = jnp.concatenate(ys, axis=1) * _silu(zg)
    y = _rms(y) * gn_ref[...]
    y_ref[...] = y[:cs, :]

    @pl.when(c == pl.num_programs(1) - 1)
    def _():
        sn_ref[...] = state_ref[...]


def _ssd(xbc, z, dtr, s0, c0, wconv, bconv, dtb, alog, dskip, gn, batch):
    rows = xbc.shape[0]
    l = rows // batch
    cs = min(SSD_CHUNK, l)
    nc = l // cs
    pairs = SSD_WIDTH // LANES
    s0 = s0.reshape(batch, pairs, 2 * SSD_HEAD_DIM, SSD_STATE)
    i = lax.broadcasted_iota(jnp.int32, (SSD_CHUNK, SSD_CHUNK), 0)
    j = lax.broadcasted_iota(jnp.int32, (SSD_CHUNK, SSD_CHUNK), 1)
    tril = (j <= i).astype(BF16)
    eh = lax.broadcasted_iota(jnp.int32, (LANES, SSD_WIDTH), 0)
    ec = lax.broadcasted_iota(jnp.int32, (LANES, SSD_WIDTH), 1) // SSD_HEAD_DIM
    expand = (eh == ec).astype(BF16)
    row = lambda n: pl.BlockSpec((cs, n), lambda b, c: (b * nc + c, 0))
    vec = lambda a: pl.BlockSpec(a.shape, lambda b, c: (0, 0))
    st = pl.BlockSpec((None, pairs, 2 * SSD_HEAD_DIM, SSD_STATE), lambda b, c: (b, 0, 0, 0))
    cv = pl.BlockSpec((None, SSD_CONV - 1, CONV_DIM), lambda b, c: (b, 0, 0))
    y, sn, cn = pl.pallas_call(
        _ssd_kernel,
        grid=(batch, nc),
        in_specs=[row(CONV_DIM), row(SSD_WIDTH), row(LANES), st, cv,
                  vec(wconv), vec(bconv), vec(dtb), vec(alog), vec(dskip), vec(gn), vec(tril), vec(expand)],
        out_specs=(row(SSD_WIDTH), st, cv),
        out_shape=(jax.ShapeDtypeStruct((rows, SSD_WIDTH), F32),
                   jax.ShapeDtypeStruct(s0.shape, F32),
                   jax.ShapeDtypeStruct((batch, SSD_CONV - 1, CONV_DIM), F32)),
        scratch_shapes=[pltpu.VMEM((8 + cs, CONV_DIM), F32),
                        pltpu.VMEM((pairs, 2 * SSD_HEAD_DIM, SSD_STATE), F32)],
        compiler_params=_params("parallel", "arbitrary"),
        name="ssd",
    )(xbc, z, dtr, s0, c0, wconv, bconv, dtb, alog, dskip, gn, tril, expand)
    return y, sn.reshape(batch, SSD_HEADS, SSD_HEAD_DIM, SSD_STATE), cn


def _out_proj_kernel(o_ref, y_ref, x_ref, woa_ref, wob_ref, gpost_ref, gpre_ref, h_ref, hn_ref):
    mix = _dot(o_ref[...].astype(BF16), woa_ref[...]) + _dot(y_ref[...].astype(BF16), wob_ref[...])
    h = x_ref[...] + _rms(mix) * gpost_ref[...]
    h_ref[...] = h
    hn_ref[...] = (_rms(h) * gpre_ref[...]).astype(BF16)


def _out_proj(o, y, x, woa, wob, gpost, gpre):
    rows, d = x.shape
    tm = min(ROW_TILE, rows)
    row = lambda n: pl.BlockSpec((tm, n), lambda i: (i, 0))
    return pl.pallas_call(
        _out_proj_kernel,
        grid=(rows // tm,),
        in_specs=[row(ATT_WIDTH), row(SSD_WIDTH), row(d),
                  _const_spec(woa.shape), _const_spec(wob.shape), _const_spec(gpost.shape), _const_spec(gpre.shape)],
        out_specs=(row(d), row(d)),
        out_shape=(jax.ShapeDtypeStruct((rows, d), F32), jax.ShapeDtypeStruct((rows, d), BF16)),
        compiler_params=_params("parallel"),
        name="out_proj",
    )(o, y, x, woa, wob, gpost, gpre)


def _gelu_tanh(x):
    return 0.5 * x * (1.0 + jnp.tanh(0.7978845608028654 * (x + 0.044715 * (x * x * x))))


def _ffn_kernel(*refs, seq_len):
    if seq_len is None:
        hn_ref, halo_ref, wg_ref, wv_ref, cwg_ref, cwv_ref, cbg_ref, cbv_ref, wd_ref, f_ref, ug_ref, uv_ref, xe_ref = refs
    else:
        hn_ref, stg_ref, stv_ref, wg_ref, wv_ref, cwg_ref, cwv_ref, cbg_ref, cbv_ref, wd_ref, f_ref, ug_ref, uv_ref = refs
    i = pl.program_id(0)
    j = pl.program_id(1)
    tm = hn_ref.shape[0]

    @pl.when(j == 0)
    def _():
        f_ref[...] = jnp.zeros_like(f_ref)

    if seq_len is None:
        @pl.when(j == 0)
        def _():
            halo = halo_ref[...]
            xe_ref[0:FFN_HALO, :] = jnp.where(i > 0, halo, jnp.zeros_like(halo))
            xe_ref[FFN_HALO:, :] = hn_ref[...]

        def conv(w_ref, cw_ref, cb_ref, u_ref):
            up = _dot(xe_ref[...], w_ref[...])
            cw = cw_ref[...]
            y = (cb_ref[...] + pltpu.roll(up, 2, 0)[FFN_HALO:] * cw[0:1]
                 + pltpu.roll(up, 1, 0)[FFN_HALO:] * cw[1:2] + up[FFN_HALO:] * cw[2:3])
            u_ref[...] = up[FFN_HALO + tm - 8:]
            return y
        gate = conv(wg_ref, cwg_ref, cbg_ref, ug_ref)
        val = conv(wv_ref, cwv_ref, cbv_ref, uv_ref)
    else:
        t = lax.broadcasted_iota(jnp.int32, (tm, 1), 0) % seq_len

        def conv(w_ref, cw_ref, cb_ref, st_ref, u_ref):
            up = _dot(hn_ref[...], w_ref[...])
            st = st_ref[...]
            cw = cw_ref[...]
            prev2 = jnp.where(t < 2, st, pltpu.roll(up, 2, 0))
            prev1 = jnp.where(t < 1, pltpu.roll(st, tm - 1, 0), pltpu.roll(up, 1, 0))
            u_ref[...] = up
            return cb_ref[...] + prev2 * cw[0:1] + prev1 * cw[1:2] + up * cw[2:3]
        gate = conv(wg_ref, cwg_ref, cbg_ref, stg_ref, ug_ref)
        val = conv(wv_ref, cwv_ref, cbv_ref, stv_ref, uv_ref)

    f_ref[...] += _dot((_gelu_tanh(gate) * val).astype(BF16), wd_ref[...])


def _ffn(hn, w_up, cw, cb, w_down, state=None, seq_len=None):
    rows, d = hn.shape
    dff = w_down.shape[0]
    tf = FFN_TF
    nj = dff // tf
    tm = min(FFN_TM, rows)
    gcol = lambda r: pl.BlockSpec((r, tf), lambda i, j: (0, j))
    vcol = lambda r: pl.BlockSpec((r, tf), lambda i, j: (0, j + nj))
    hn_spec = pl.BlockSpec((tm, d), lambda i, j: (i, 0))
    common = [pl.BlockSpec((d, tf), lambda i, j: (0, j)), pl.BlockSpec((d, tf), lambda i, j: (0, j + nj)),
              gcol(FFN_CONV), vcol(FFN_CONV), gcol(1), vcol(1),
              pl.BlockSpec((tf, d), lambda i, j: (j, 0))]
    common_args = (w_up, w_up, cw, cw, cb, cb, w_down)
    f_spec = pl.BlockSpec((tm, d), lambda i, j: (i, 0))
    if seq_len is None:
        per = tm // FFN_HALO
        halo = pl.BlockSpec((FFN_HALO, d), lambda i, j: (jnp.maximum(i * per - 1, 0), 0))
        in_specs = [hn_spec, halo] + common
        args = (hn, hn) + common_args
        u_rows = 8 * (rows // tm)
        u_spec = pl.BlockSpec((8, tf), lambda i, j: (i, j))
        scratch = [pltpu.VMEM((FFN_HALO + tm, d), BF16)]
    else:
        assert rows == tm
        in_specs = [hn_spec, pl.BlockSpec((tm, tf), lambda i, j: (0, j)),
                    pl.BlockSpec((tm, tf), lambda i, j: (0, j + nj))] + common
        args = (hn, state, state) + common_args
        u_rows = tm
        u_spec = pl.BlockSpec((u_rows, tf), lambda i, j: (0, j))
        scratch = []
    return pl.pallas_call(
        functools.partial(_ffn_kernel, seq_len=seq_len),
        grid=(rows // tm, nj),
        in_specs=in_specs,
        out_specs=(f_spec, u_spec, u_spec),
        out_shape=(jax.ShapeDtypeStruct((rows, d), F32),
                   jax.ShapeDtypeStruct((u_rows, dff), F32),
                   jax.ShapeDtypeStruct((u_rows, dff), F32)),
        scratch_shapes=scratch,
        compiler_params=_params("arbitrary", "arbitrary"),
        name="ffn",
    )(*args)


def _ple_kernel(h_ref, f_ref, p_ref, gpost_ref, wg_ref, wp_ref, o_ref):
    h = h_ref[...] + _rms(f_ref[...]) * gpost_ref[...]
    gate = 1.0 / (1.0 + jnp.exp(-_dot(h.astype(BF16), wg_ref[...])))
    o_ref[...] = h + _dot(p_ref[...].astype(BF16), wp_ref[...]) * gate


def _ple(h, f, p, gpost, wg, wp):
    rows, d = h.shape
    tm = min(ROW_TILE, rows)
    row = lambda n: pl.BlockSpec((tm, n), lambda i: (i, 0))
    return pl.pallas_call(
        _ple_kernel,
        grid=(rows // tm,),
        in_specs=[row(d), row(d), row(p.shape[1]),
                  _const_spec(gpost.shape), _const_spec(wg.shape), _const_spec(wp.shape)],
        out_specs=row(d),
        out_shape=jax.ShapeDtypeStruct((rows, d), F32),
        compiler_params=_params("parallel"),
        name="ple",
    )(h, f, p, gpost, wg, wp)


def _after_attention(x, p, lw, batch, o, kf, vf, z, xbc, dtr, ssm0, conv0, ffn0):
    rows, d = x.shape
    l = rows // batch
    sample = ffn0 is not None
    y, ssm_new, conv_new = _ssd(xbc, z, dtr, ssm0, conv0, lw["w_conv"], lw["b_conv"], lw["dt_bias"], lw["a_log"],
                                lw["d_skip"], lw["g_ssd_norm"], batch)
    h1, hn = _out_proj(o, y, x, lw["w_out_a"], lw["w_out_b"], lw["g_post_mix"], lw["g_pre_ffn"])
    dff2 = lw["w_up"].shape[1]
    if sample:
        st = jnp.pad(ffn0, ((0, 0), (0, l - (FFN_CONV - 1)), (0, 0))).reshape(rows, dff2)
        f, ug, uv = _ffn(hn, lw["w_up"], lw["w_ffn_conv"], lw["b_ffn_conv"], lw["w_down"], state=st, seq_len=l)
        up = jnp.concatenate([ug, uv], axis=1).reshape(batch, l, dff2)
        ffn_new = up[:, l - (FFN_CONV - 1):, :]
    else:
        f, ug, uv = _ffn(hn, lw["w_up"], lw["w_ffn_conv"], lw["b_ffn_conv"], lw["w_down"])
        ffn_new = jnp.concatenate([ug, uv], axis=1)[None, -(FFN_CONV - 1):, :]
    out = _ple(h1, f, p, lw["g_post_ffn"], lw["w_ple_gate"], lw["w_ple_proj"])
    return out, (kf, vf, ssm_new, conv_new, ffn_new)


def _prep_weights(i, g_pre_mix, w_in, sb_bias, w_conv, b_conv, dt_bias, a_log, d_skip, g_ssd_norm, w_out,
                  g_post_mix, g_pre_ffn, w_up, w_ffn_conv, b_ffn_conv, w_down, g_post_ffn, w_ple_gate, w_ple_proj):
    row = lambda a: a[i][None, :].astype(F32)
    lane_pad = lambda a: jnp.pad(a[i].astype(F32), (0, LANES - a.shape[1]))[None, :]
    w = w_in[i].astype(BF16)
    o_dt = 3 * ATT_WIDTH + SSD_WIDTH + CONV_DIM
    return {
        "g_pre_mix": row(g_pre_mix),
        "w_in": w,
        "w_dt": jnp.pad(w[:, o_dt:], ((0, 0), (0, LANES - SSD_HEADS))),
        "sb_bias": sb_bias[i].astype(F32),
        "w_conv": w_conv[i].astype(F32), "b_conv": row(b_conv),
        "dt_bias": lane_pad(dt_bias), "a_log": lane_pad(a_log),
        "d_skip": jnp.repeat(d_skip[i].astype(F32), SSD_HEAD_DIM)[None, :],
        "g_ssd_norm": row(g_ssd_norm),
        "w_out_a": w_out[i, :ATT_WIDTH].astype(BF16), "w_out_b": w_out[i, ATT_WIDTH:].astype(BF16),
        "g_post_mix": row(g_post_mix), "g_pre_ffn": row(g_pre_ffn),
        "w_up": w_up[i].astype(BF16), "w_ffn_conv": w_ffn_conv[i].astype(F32), "b_ffn_conv": row(b_ffn_conv),
        "w_down": w_down[i].astype(BF16), "g_post_ffn": row(g_post_ffn),
        "w_ple_gate": w_ple_gate[i].astype(BF16), "w_ple_proj": w_ple_proj[i].astype(BF16),
    }


def kernel(x_prompt, x_sample, cache_k, cache_v, state_ssm, state_conv, state_ffn_conv, page_table, p_prompt, p_sample, g_pre_mix, w_in, sb_bias, w_conv, b_conv, dt_bias, a_log, d_skip, g_ssd_norm, w_out, g_post_mix, g_pre_ffn, w_up, w_ffn_conv, b_ffn_conv, w_down, g_post_ffn, w_ple_gate, w_ple_proj):
    bp, lp, d = x_prompt.shape
    db, ls, _ = x_sample.shape
    depth = w_in.shape[0]
    dff2 = w_up.shape[2]
    hp = x_prompt.reshape(bp * lp, d)
    hs = x_sample.reshape(db * ls, d)
    outs_p, outs_s = [], []
    for i in range(depth):
        lw = _prep_weights(i, g_pre_mix, w_in, sb_bias, w_conv, b_conv, dt_bias, a_log, d_skip, g_ssd_norm, w_out,
                           g_post_mix, g_pre_ffn, w_up, w_ffn_conv, b_ffn_conv, w_down, g_post_ffn,
                           w_ple_gate, w_ple_proj)
        ssm0 = jnp.zeros((bp, SSD_HEADS, SSD_HEAD_DIM, SSD_STATE), F32)
        conv0 = jnp.zeros((bp, SSD_CONV - 1, CONV_DIM), F32)
        proj = lambda x, q_dtype: _in_proj(x, lw["g_pre_mix"], lw["w_in"], lw["w_dt"], q_dtype)
        q_p, kf_p, vf_p, kb_p, vb_p, z_p, xbc_p, dt_p = proj(hp, BF16)
        q_s, kf_s, vf_s, _, _, z_s, xbc_s, dt_s = proj(hs, F32)
        seq = lambda a: a.reshape(db, ls, ATT_WIDTH)
        o_p, o_s = _attention(lw["sb_bias"], page_table, q_p, kb_p, vb_p, seq(q_s), seq(kf_s), seq(vf_s),
                              cache_k[i], cache_v[i])
        hp, st_p = _after_attention(hp, p_prompt[i].reshape(bp * lp, -1), lw, bp, o_p, kf_p, vf_p, z_p, xbc_p, dt_p,
                                    ssm0, conv0, None)
        hs, st_s = _after_attention(hs, p_sample[i].reshape(db * ls, -1), lw, db, o_s.reshape(db * ls, ATT_WIDTH),
                                    kf_s, vf_s, z_s, xbc_s, dt_s, state_ssm[i], state_conv[i], state_ffn_conv[i])
        outs_p.append(st_p)
        outs_s.append(st_s)

    def stack(outs, j, shape):
        return jnp.stack([o[j].reshape(shape) for o in outs], axis=0)

    kv_p = (bp, lp, ATT_HEADS, ATT_HEAD_DIM)
    kv_s = (db, ls, ATT_HEADS, ATT_HEAD_DIM)
    ssm_p = (bp, SSD_HEADS, SSD_HEAD_DIM, SSD_STATE)
    ssm_s = (db, SSD_HEADS, SSD_HEAD_DIM, SSD_STATE)
    return (hp.reshape(bp, lp, d), hs.reshape(db, ls, d),
            stack(outs_p, 0, kv_p), stack(outs_s, 0, kv_s), stack(outs_p, 1, kv_p), stack(outs_s, 1, kv_s),
            stack(outs_p, 2, ssm_p), stack(outs_s, 2, ssm_s),
            stack(outs_p, 3, (bp, SSD_CONV - 1, CONV_DIM)), stack(outs_s, 3, (db, SSD_CONV - 1, CONV_DIM)),
            stack(outs_p, 4, (bp, FFN_CONV - 1, dff2)), stack(outs_s, 4, (db, FFN_CONV - 1, dff2)))
```

```python
import functools

import jax
import jax.numpy as jnp
import numpy as np
from jax import lax
from jax.experimental import pallas as pl
from jax.experimental.pallas import tpu as pltpu

F32 = jnp.float32
BF16 = jnp.bfloat16

ATT_HEADS = 16
ATT_HEAD_DIM = 64
ATT_WIDTH = ATT_HEADS * ATT_HEAD_DIM
SSD_HEADS = 16
SSD_HEAD_DIM = 64
SSD_WIDTH = SSD_HEADS * SSD_HEAD_DIM
SSD_GROUPS = 2
SSD_STATE = 128
SSD_CONV = 4
SSD_CHUNK = 128
CONV_DIM = SSD_WIDTH + 2 * SSD_GROUPS * SSD_STATE
FFN_CONV = 3
RMS_EPS = 1e-6
PAGE_SIZE = 128

LANES = 128
HEAD_PAIRS = ATT_WIDTH // LANES
VMEM_LIMIT = 56 * 1024 * 1024

ROW_TILE = 512
IN_PROJ_TILE = 256
ATT_TQ = 512
ATT_TK = 256
FFN_TM = 1024
FFN_TF = 512
FFN_HALO = 16
PAGES_PER_STEP = 4
PAGE_SLOTS = 3


def _params(*sem):
    return pltpu.CompilerParams(dimension_semantics=sem, vmem_limit_bytes=VMEM_LIMIT)


def _rms(x):
    return x * lax.rsqrt(jnp.mean(x * x, axis=-1, keepdims=True) + RMS_EPS)


def _softplus(x):
    return jnp.maximum(x, 0.0) + jnp.log(1.0 + jnp.exp(-jnp.abs(x)))


def _silu(x):
    return x * (1.0 / (1.0 + jnp.exp(-x)))


def _dot(a, b):
    return jnp.dot(a, b, preferred_element_type=F32)


def _dot_nt(a, b):
    return lax.dot_general(a, b, (((1,), (1,)), ((), ())), preferred_element_type=F32)


def _dot_tn(a, b):
    return lax.dot_general(a, b, (((0,), (0,)), ((), ())), preferred_element_type=F32)


def _split3(x):
    hi = x.astype(BF16)
    r1 = x - hi.astype(F32)
    mid = r1.astype(BF16)
    lo = (r1 - mid.astype(F32)).astype(BF16)
    return hi, mid, lo


def _dot_exact_rhs(m, x):
    hi, mid, lo = _split3(x)
    return _dot(m, hi) + _dot(m, mid) + _dot(m, lo)


def _dot_exact_lhs(x, m):
    hi, mid, lo = _split3(x)
    return _dot(hi, m) + _dot(mid, m) + _dot(lo, m)


def _const_spec(shape):
    return pl.BlockSpec(shape, lambda *_: (0,) * len(shape))


def _in_proj_kernel(x_ref, g_ref, w_ref, wdt_ref,
                    q_ref, kf_ref, vf_ref, kb_ref, vb_ref, z_ref, xbc_ref, dt_ref):
    o_z = 3 * ATT_WIDTH
    o_x = o_z + SSD_WIDTH
    u = (_rms(x_ref[...]) * g_ref[...]).astype(BF16)
    qkv = _dot(u, w_ref[:, :o_z])
    q_ref[...] = (qkv[:, :ATT_WIDTH] * (ATT_HEAD_DIM ** -0.5)).astype(q_ref.dtype)
    k = qkv[:, ATT_WIDTH:2 * ATT_WIDTH]
    v = qkv[:, 2 * ATT_WIDTH:]
    kf_ref[...] = k
    vf_ref[...] = v
    kb_ref[...] = k.astype(BF16)
    vb_ref[...] = v.astype(BF16)
    z_ref[...] = _dot(u, w_ref[:, o_z:o_x])
    xbc_ref[...] = _dot(u, w_ref[:, o_x:o_x + CONV_DIM])
    dt_ref[...] = _dot(u, wdt_ref[...])


def _in_proj(x, g, w, wdt, q_dtype):
    rows, d = x.shape
    tm = min(IN_PROJ_TILE, rows)
    row = lambda n: pl.BlockSpec((tm, n), lambda i: (i, 0))
    res = lambda a: pl.BlockSpec(a.shape, lambda i: (0, 0), pipeline_mode=pl.Buffered(1))
    out_shape = (
        jax.ShapeDtypeStruct((rows, ATT_WIDTH), q_dtype),
        jax.ShapeDtypeStruct((rows, ATT_WIDTH), F32),
        jax.ShapeDtypeStruct((rows, ATT_WIDTH), F32),
        jax.ShapeDtypeStruct((rows, ATT_WIDTH), BF16),
        jax.ShapeDtypeStruct((rows, ATT_WIDTH), BF16),
        jax.ShapeDtypeStruct((rows, SSD_WIDTH), F32),
        jax.ShapeDtypeStruct((rows, CONV_DIM), F32),
        jax.ShapeDtypeStruct((rows, LANES), F32),
    )
    return pl.pallas_call(
        _in_proj_kernel,
        grid=(rows // tm,),
        in_specs=[row(d), res(g), res(w), res(wdt)],
        out_specs=tuple(row(s.shape[1]) for s in out_shape),
        out_shape=out_shape,
        compiler_params=_params("parallel"),
        name="in_proj",
    )(x, g, w, wdt)


def _stick_block(z, carry, tri, mask):
    rows, tk = z.shape
    sp = _softplus(z)
    if mask is not None:
        sp = jnp.where(mask, sp, 0.0)
    cs = _dot(sp.astype(BF16), tri)
    wide = carry if tk == LANES else jnp.concatenate([carry] * (tk // LANES), axis=1)
    p = jnp.exp(z - cs[:, :tk] - wide)
    if mask is not None:
        p = jnp.where(mask, p, 0.0)
    total = cs[:, tk:] if tri.shape[1] > tk else jnp.broadcast_to(cs[:, 0:1], (rows, LANES))
    return p, carry + total


def _tri(tk, with_ones):
    n = tk + LANES if with_ones else tk
    s = lax.broadcasted_iota(jnp.int32, (tk, n), 0)
    j = lax.broadcasted_iota(jnp.int32, (tk, n), 1)
    return ((s >= j) | (j >= tk)).astype(BF16)


def _prompt_stages(qi, q_ref, qfill_ref, kfill_ref, k_ref, v_ref, tri_ref, acc_ref, c_ref, z_ref, p_ref):
    tq, tk = ATT_TQ, ATT_TK
    n_blocks = (qi + 1) * (tq // tk)
    low = lax.broadcasted_iota(jnp.int32, (1, LANES), 1) < ATT_HEAD_DIM
    own = (low, jnp.logical_not(low))
    q = q_ref[...]
    qs = tuple(jnp.where(own[a], q, qfill_ref[a]) for a in range(2))
    tri = tri_ref[...]
    q_pos = qi * tq + lax.broadcasted_iota(jnp.int32, (tq, tk), 0)
    k_off = lax.broadcasted_iota(jnp.int32, (tq, tk), 1)

    def rows(n):
        return pl.ds(pl.multiple_of(jnp.maximum(n_blocks - 1 - n, 0) * tk, tk), tk)

    def scores(n, slot):
        kb = k_ref[rows(n), :]
        for a in range(2):
            z_ref[slot, a] = _dot_nt(qs[a], jnp.where(own[a], kb, kfill_ref[a]))

    def weights(n, slot, masked):
        mask = ((n_blocks - 1 - n) * tk + k_off) < q_pos if masked else None
        for a in range(2):
            p, c_ref[a] = _stick_block(z_ref[slot, a], c_ref[a], tri, mask)
            p_ref[slot, :, a * tk:(a + 1) * tk] = p.astype(BF16)

    def apply(n, slot):
        vb = v_ref[rows(n), :]
        vz = jnp.zeros_like(vb)
        vcat = jnp.concatenate([jnp.where(low, vb, vz), jnp.where(low, vz, vb)], axis=0)
        acc_ref[...] += _dot(p_ref[slot], vcat)

    def pair(u, masked, first):
        n = 2 * u
        scores(n + 1, 1)
        if not first:
            apply(n - 1, 1)
        weights(n, 0, masked)
        scores(n + 2, 0)
        apply(n, 0)
        weights(n + 1, 1, masked)

    def start():
        acc_ref[...] = jnp.zeros_like(acc_ref)
        c_ref[...] = jnp.zeros_like(c_ref)
        scores(0, 0)

    def finish(o_ref):
        apply(n_blocks - 1, 1)
        o_ref[...] = acc_ref[...].astype(o_ref.dtype)

    return start, pair, finish


def _bias_lanes(bias):
    b = bias.astype(F32)
    b0 = b.astype(BF16)
    r = b - b0.astype(F32)
    b1 = r.astype(BF16)
    b2 = (r - b1.astype(F32)).astype(BF16)
    terms = jnp.stack([b0, b1, b2], axis=-1).reshape(HEAD_PAIRS, 2, 3)
    pad = jnp.zeros((HEAD_PAIRS, ATT_HEAD_DIM - 3), BF16)
    zero = jnp.zeros((HEAD_PAIRS, ATT_HEAD_DIM), BF16)
    first = jnp.concatenate([zero, terms[:, 0], pad], axis=1)
    second = jnp.concatenate([terms[:, 1], pad, zero], axis=1)
    qfill = jnp.stack([first, second], axis=1)
    ones = jnp.concatenate([jnp.ones((3,), BF16), jnp.zeros((ATT_HEAD_DIM - 3,), BF16)])
    zeros = jnp.zeros((ATT_HEAD_DIM,), BF16)
    kfill = jnp.stack([jnp.concatenate([zeros, ones]), jnp.concatenate([ones, zeros])])
    return qfill, kfill


def _sample_stages(q_ref, kn_ref, vn_ref, brow_ref, tri_ref, k_refs, v_refs, qexp_ref, acc_ref, c_ref):
    n = len(k_refs)
    rows = ATT_HEADS * q_ref.shape[0]
    lq = q_ref.shape[0]
    col_head = lax.broadcasted_iota(jnp.int32, (1, ATT_WIDTH), 1) // ATT_HEAD_DIM
    tri = tri_ref[...]
    brow = brow_ref[...]

    def begin():
        qt = jnp.concatenate([q_ref[...]] * ATT_HEADS, axis=0)
        row_head = lax.broadcasted_iota(jnp.int32, (rows, 1), 0) // lq
        qexp_ref[...] = jnp.where(row_head == col_head, qt, 0.0).astype(BF16)
        c_ref[...] = jnp.zeros_like(c_ref)
        pad = jnp.zeros((PAGE_SIZE - lq, ATT_WIDTH), F32)
        kn = jnp.concatenate([kn_ref[...], pad], axis=0)
        vn = jnp.concatenate([vn_ref[...], pad], axis=0)
        t = lax.broadcasted_iota(jnp.int32, (rows, PAGE_SIZE), 0) % lq
        tok = lax.broadcasted_iota(jnp.int32, (rows, PAGE_SIZE), 1)
        z = _dot_nt(qexp_ref[...], kn.astype(BF16)) + brow
        p, c_ref[...] = _stick_block(z, jnp.zeros_like(brow), tri, tok < t)
        acc_ref[...] = _dot(p.astype(BF16), vn.astype(BF16))

    def pages():
        kcat = jnp.concatenate([k_refs[i][...].astype(BF16) for i in reversed(range(n))], axis=1)
        vcat = jnp.concatenate([v_refs[i][...].astype(BF16) for i in reversed(range(n))], axis=1)
        z = _dot(qexp_ref[...], kcat)
        c = c_ref[...]
        ps = [None] * n
        for i in reversed(range(n)):
            zi = z[:, i * PAGE_SIZE:(i + 1) * PAGE_SIZE] + brow
            p, c = _stick_block(zi, c, tri, None)
            ps[i] = p.astype(BF16)
        c_ref[...] = c
        acc_ref[...] += _dot_nt(jnp.concatenate(ps, axis=1), vcat)

    def end(o_ref):
        out = jnp.zeros((lq, ATT_WIDTH), F32)
        for h in range(ATT_HEADS):
            out = jnp.where(col_head == h, acc_ref[h * lq:(h + 1) * lq, :], out)
        o_ref[...] = out

    return begin, pages, end


SEQ_FIRST, SEQ_MIDDLE, SEQ_LAST, SEQ_NONE = 0, 1, 2, 3


def _attn_kernel(hp_tab, qi_tab, u_tab, seq_tab, role_tab, page_tab,
                 q_ref, qfill_ref, kfill_ref, k_ref, v_ref, trip_ref,
                 qs_ref, kn_ref, vn_ref, brow_ref, tris_ref, ck_ref, cv_ref,
                 op_ref, os_ref, acc_ref, c_ref, z_ref, p_ref, qexp_ref, sacc_ref, sc_ref, kbuf_ref, vbuf_ref, sem,
                 *, page_steps):
    n = PAGES_PER_STEP
    s = pl.program_id(0)
    qi = qi_tab[s]
    u = u_tab[s]
    role = role_tab[s]
    slot = lax.rem(s, PAGE_SLOTS)

    def copies(step, into):
        return [pltpu.make_async_copy(src.at[page_tab[step * n + i]], buf.at[into, i], sem.at[into])
                for src, buf in ((ck_ref, kbuf_ref), (cv_ref, vbuf_ref)) for i in range(n)]

    @pl.when(s == 0)
    def _():
        for step in range(PAGE_SLOTS - 1):
            for c in copies(step, step):
                c.start()

    ahead = s + PAGE_SLOTS - 1

    @pl.when(ahead < page_steps)
    def _():
        for c in copies(ahead, lax.rem(ahead, PAGE_SLOTS)):
            c.start()

    @pl.when(s < page_steps)
    def _():
        for c in copies(s, slot):
            c.wait()

    k_refs = [kbuf_ref.at[slot, i] for i in range(n)]
    v_refs = [vbuf_ref.at[slot, i] for i in range(n)]
    start, pair, finish = _prompt_stages(qi, q_ref, qfill_ref, kfill_ref, k_ref, v_ref, trip_ref,
                                         acc_ref, c_ref, z_ref, p_ref)
    begin, pages, end = _sample_stages(qs_ref, kn_ref, vn_ref, brow_ref, tris_ref, k_refs, v_refs,
                                       qexp_ref, sacc_ref, sc_ref)

    @pl.when(role == SEQ_FIRST)
    def _():
        begin()

    @pl.when(u == 0)
    def _():
        start()
        pair(0, True, True)
        pages()

    @pl.when(jnp.logical_and(u > 0, role != SEQ_NONE))
    def _():
        pair(u, False, False)
        pages()

    @pl.when(jnp.logical_and(u > 0, role == SEQ_NONE))
    def _():
        pair(u, False, False)

    @pl.when(u == qi)
    def _():
        finish(op_ref)

    @pl.when(role == SEQ_LAST)
    def _():
        end(os_ref)


def _attention(bias, page_table, q_p, k_p, v_p, q_s, k_new, v_new, cache_k, cache_v):
    l = q_p.shape[0]
    db, lq, _ = q_s.shape
    n_pages = page_table.shape[1]
    n = PAGES_PER_STEP
    tq, tk = ATT_TQ, ATT_TK
    nq = l // tq
    assert tq == 2 * tk and l % tq == 0 and n_pages % n == 0
    per_seq = n_pages // n
    page_steps = db * per_seq
    steps = [(hp, qi, u) for hp in range(HEAD_PAIRS) for qi in range(nq) for u in range(qi + 1)]
    assert len(steps) >= page_steps, "the sample page stream needs at least one grid step per page step"
    assert per_seq >= 2
    hp_tab, qi_tab, u_tab = (jnp.asarray([t[i] for t in steps], jnp.int32) for i in range(3))
    step = np.arange(len(steps))
    ps = np.minimum(step, page_steps - 1)
    within = ps % per_seq
    role = np.where(step >= page_steps, SEQ_NONE,
                    np.where(within == 0, SEQ_FIRST, np.where(within == per_seq - 1, SEQ_LAST, SEQ_MIDDLE)))
    cols = n_pages - 1 - (within[:, None] * n + np.arange(n)[None, :])
    seq_tab = jnp.asarray(ps // per_seq, jnp.int32)
    role_tab = jnp.asarray(role, jnp.int32)
    page_tab = page_table[(ps // per_seq)[:, None], cols].reshape(-1).astype(jnp.int32)
    qfill, kfill = _bias_lanes(bias)
    qfill = jnp.broadcast_to(qfill[:, :, None, :], (HEAD_PAIRS, 2, tq, LANES))
    kfill = jnp.broadcast_to(kfill[:, None, :], (2, tk, LANES))
    rows = ATT_HEADS * lq
    pool = cache_k.shape[0]
    ck = jnp.transpose(cache_k, (0, 2, 3, 1)).reshape(pool, ATT_WIDTH, PAGE_SIZE)
    cv = jnp.transpose(cache_v, (0, 2, 3, 1)).reshape(pool, ATT_WIDTH, PAGE_SIZE)
    brow = jnp.broadcast_to(jnp.repeat(bias.astype(F32), lq)[:, None], (rows, LANES))

    const = lambda shape: pl.BlockSpec(shape, lambda s, *tabs: (0,) * len(shape))
    tile = pl.BlockSpec((tq, LANES), lambda s, hp, qi, *tabs: (qi[s], hp[s]))
    whole = pl.BlockSpec((l, LANES), lambda s, hp, *tabs: (0, hp[s]))
    seq = pl.BlockSpec((None, lq, ATT_WIDTH), lambda s, hp, qi, u, sq, *tabs: (sq[s], 0, 0))
    cache = pl.BlockSpec(memory_space=pl.ANY)
    assert page_steps >= PAGE_SLOTS

    grid_spec = pltpu.PrefetchScalarGridSpec(
        num_scalar_prefetch=6,
        grid=(len(steps),),
        in_specs=[tile,
                  pl.BlockSpec((None, 2, tq, LANES), lambda s, hp, *tabs: (hp[s], 0, 0, 0)),
                  const((2, tk, LANES)), whole, whole, const((tk, tk)),
                  seq, seq, seq, const((rows, LANES)), const((PAGE_SIZE, PAGE_SIZE + LANES)), cache, cache],
        out_specs=(tile, seq),
        scratch_shapes=[pltpu.VMEM((tq, LANES), F32), pltpu.VMEM((2, tq, LANES), F32),
                        pltpu.VMEM((2, 2, tq, tk), F32), pltpu.VMEM((2, tq, 2 * tk), BF16),
                        pltpu.VMEM((rows, ATT_WIDTH), BF16), pltpu.VMEM((rows, ATT_WIDTH), F32),
                        pltpu.VMEM((rows, LANES), F32),
                        pltpu.VMEM((PAGE_SLOTS, n, ATT_WIDTH, PAGE_SIZE), F32),
                        pltpu.VMEM((PAGE_SLOTS, n, ATT_WIDTH, PAGE_SIZE), F32),
                        pltpu.SemaphoreType.DMA((PAGE_SLOTS,))],
    )
    return pl.pallas_call(
        functools.partial(_attn_kernel, page_steps=page_steps),
        grid_spec=grid_spec,
        out_shape=(jax.ShapeDtypeStruct((l, ATT_WIDTH), BF16), jax.ShapeDtypeStruct((db, lq, ATT_WIDTH), F32)),
        compiler_params=_params("arbitrary"),
        name="attention",
    )(hp_tab, qi_tab, u_tab, seq_tab, role_tab, page_tab, q_p, qfill, kfill, k_p, v_p, _tri(tk, False),
      q_s, k_new, v_new, brow, _tri(PAGE_SIZE, True), ck, cv)


def _ssd_kernel(xbc_ref, z_ref, dtr_ref, s0_ref, c0_ref, wconv_ref, bconv_ref, dtb_ref, alog_ref,
                dskip_ref, gn_ref, tril_ref, e_ref,
                y_ref, sn_ref, cn_ref, cbuf_ref, state_ref):
    c = pl.program_id(1)
    cs = xbc_ref.shape[0]
    ck = SSD_CHUNK
    tail = 8
    hist = SSD_CONV - 1

    @pl.when(c == 0)
    def _():
        state_ref[...] = s0_ref[...]
        cbuf_ref[0:tail, :] = jnp.zeros((tail, CONV_DIM), F32)
        cbuf_ref[tail - hist:tail, :] = c0_ref[...]

    cbuf_ref[tail:tail + cs, :] = xbc_ref[...]
    w = wconv_ref[...]
    xall = cbuf_ref[...]
    xc = bconv_ref[...] + pltpu.roll(xall, hist, 0)[tail:] * w[0:1]
    for j in range(1, SSD_CONV):
        tap = xall if j == hist else pltpu.roll(xall, hist - j, 0)
        xc = xc + tap[tail:] * w[j:j + 1]
    cn_ref[...] = cbuf_ref[tail + cs - hist:tail + cs, :]
    last = cbuf_ref[cs:cs + tail, :]
    cbuf_ref[0:tail, :] = last
    xc = _silu(xc)
    dt = _softplus(dtr_ref[...] + dtb_ref[...])
    zg = z_ref[...]
    if cs < ck:
        xc = jnp.concatenate([xc, jnp.zeros((ck - cs, CONV_DIM), F32)], axis=0)
        dt = jnp.concatenate([dt, jnp.zeros((ck - cs, LANES), F32)], axis=0)
        zg = jnp.concatenate([zg, jnp.zeros((ck - cs, SSD_WIDTH), F32)], axis=0)

    a = -jnp.exp(alog_ref[...])
    acum = _dot_exact_rhs(tril_ref[...], dt * a)
    acum_row = acum.T
    e = e_ref[...]
    dt_x = _dot_exact_lhs(dt, e)
    ac_x = _dot_exact_lhs(acum, e)
    xs = xc[:, :SSD_WIDTH]
    xd = xs * dt_x
    xdw = (xd * jnp.exp(ac_x[ck - 1:ck, :] - ac_x)).astype(BF16)
    eac_x = jnp.exp(ac_x)
    row_i = lax.broadcasted_iota(jnp.int32, (ck, ck), 0)
    col_j = lax.broadcasted_iota(jnp.int32, (ck, ck), 1)
    causal = row_i >= col_j
    low = lax.broadcasted_iota(jnp.int32, (1, LANES), 1) < SSD_HEAD_DIM
    per_group = SSD_HEADS // SSD_GROUPS
    ys = []
    gmat = {}
    for hp in range(SSD_WIDTH // LANES):
        g = (2 * hp) // per_group
        b_g = xc[:, SSD_WIDTH + g * SSD_STATE:SSD_WIDTH + (g + 1) * SSD_STATE].astype(BF16)
        c_off = SSD_WIDTH + SSD_GROUPS * SSD_STATE
        c_g = xc[:, c_off + g * SSD_STATE:c_off + (g + 1) * SSD_STATE].astype(BF16)
        if g not in gmat:
            gmat[g] = _dot_nt(c_g, b_g)
        sl = slice(hp * LANES, (hp + 1) * LANES)
        xd_p = xd[:, sl]
        y_p = None
        decs = []
        for k in range(2):
            h = 2 * hp + k
            diff = acum[:, h:h + 1] - acum_row[h:h + 1, :]
            decay = jnp.where(causal, jnp.exp(jnp.minimum(diff, 0.0)), 0.0)
            sc = (gmat[g] * decay).astype(BF16)
            xd_k = jnp.where(low if k == 0 else jnp.logical_not(low), xd_p, 0.0).astype(BF16)
            d = _dot(sc, xd_k)
            y_p = d if y_p is None else y_p + d
            decs.append(jnp.broadcast_to(jnp.exp(acum[ck - 1:ck, h:h + 1]), (SSD_HEAD_DIM, SSD_STATE)))
        s_p = state_ref[hp]
        y_p = y_p + _dot_nt(c_g, s_p.astype(BF16)) * eac_x[:, sl]
        state_ref[hp] = s_p * jnp.concatenate(decs, axis=0) + _dot_tn(xdw[:, sl], b_g)
        ys.append(y_p + dskip_ref[:, sl] * xs[:, sl])
    y = jnp.concatenate(ys, axis=1) * _silu(zg)
    y = _rms(y) * gn_ref[...]
    y_ref[...] = y[:cs, :]

    @pl.when(c == pl.num_programs(1) - 1)
    def _():
        sn_ref[...] = state_ref[...]


def _ssd(xbc, z, dtr, s0, c0, wconv, bconv, dtb, alog, dskip, gn, batch):
    rows = xbc.shape[0]
    l = rows // batch
    cs = min(SSD_CHUNK, l)
    nc = l // cs
    pairs = SSD_WIDTH // LANES
    s0 = s0.reshape(batch, pairs, 2 * SSD_HEAD_DIM, SSD_STATE)
    i = lax.broadcasted_iota(jnp.int32, (SSD_CHUNK, SSD_CHUNK), 0)
    j = lax.broadcasted_iota(jnp.int32, (SSD_CHUNK, SSD_CHUNK), 1)
    tril = (j <= i).astype(BF16)
    eh = lax.broadcasted_iota(jnp.int32, (LANES, SSD_WIDTH), 0)
    ec = lax.broadcasted_iota(jnp.int32, (LANES, SSD_WIDTH), 1) // SSD_HEAD_DIM
    expand = (eh == ec).astype(BF16)
    row = lambda n: pl.BlockSpec((cs, n), lambda b, c: (b * nc + c, 0))
    vec = lambda a: pl.BlockSpec(a.shape, lambda b, c: (0, 0))
    st = pl.BlockSpec((None, pairs, 2 * SSD_HEAD_DIM, SSD_STATE), lambda b, c: (b, 0, 0, 0))
    cv = pl.BlockSpec((None, SSD_CONV - 1, CONV_DIM), lambda b, c: (b, 0, 0))
    y, sn, cn = pl.pallas_call(
        _ssd_kernel,
        grid=(batch, nc),
        in_specs=[row(CONV_DIM), row(SSD_WIDTH), row(LANES), st, cv,
                  vec(wconv), vec(bconv), vec(dtb), vec(alog), vec(dskip), vec(gn), vec(tril), vec(expand)],
        out_specs=(row(SSD_WIDTH), st, cv),
        out_shape=(jax.ShapeDtypeStruct((rows, SSD_WIDTH), F32),
                   jax.ShapeDtypeStruct(s0.shape, F32),
                   jax.ShapeDtypeStruct((batch, SSD_CONV - 1, CONV_DIM), F32)),
        scratch_shapes=[pltpu.VMEM((8 + cs, CONV_DIM), F32),
                        pltpu.VMEM((pairs, 2 * SSD_HEAD_DIM, SSD_STATE), F32)],
        compiler_params=_params("parallel", "arbitrary"),
        name="ssd",
    )(xbc, z, dtr, s0, c0, wconv, bconv, dtb, alog, dskip, gn, tril, expand)
    return y, sn.reshape(batch, SSD_HEADS, SSD_HEAD_DIM, SSD_STATE), cn


def _out_proj_kernel(o_ref, y_ref, x_ref, woa_ref, wob_ref, gpost_ref, gpre_ref, h_ref, hn_ref):
    mix = _dot(o_ref[...].astype(BF16), woa_ref[...]) + _dot(y_ref[...].astype(BF16), wob_ref[...])
    h = x_ref[...] + _rms(mix) * gpost_ref[...]
    h_ref[...] = h
    hn_ref[...] = (_rms(h) * gpre_ref[...]).astype(BF16)


def _out_proj(o, y, x, woa, wob, gpost, gpre):
    rows, d = x.shape
    tm = min(ROW_TILE, rows)
    row = lambda n: pl.BlockSpec((tm, n), lambda i: (i, 0))
    return pl.pallas_call(
        _out_proj_kernel,
        grid=(rows // tm,),
        in_specs=[row(ATT_WIDTH), row(SSD_WIDTH), row(d),
                  _const_spec(woa.shape), _const_spec(wob.shape), _const_spec(gpost.shape), _const_spec(gpre.shape)],
        out_specs=(row(d), row(d)),
        out_shape=(jax.ShapeDtypeStruct((rows, d), F32), jax.ShapeDtypeStruct((rows, d), BF16)),
        compiler_params=_params("parallel"),
        name="out_proj",
    )(o, y, x, woa, wob, gpost, gpre)


def _gelu_tanh(x):
    return 0.5 * x * (1.0 + jnp.tanh(0.7978845608028654 * (x + 0.044715 * (x * x * x))))


def _ffn_kernel(*refs, seq_len):
    if seq_len is None:
        hn_ref, halo_ref, wg_ref, wv_ref, cwg_ref, cwv_ref, cbg_ref, cbv_ref, wd_ref, f_ref, ug_ref, uv_ref, xe_ref = refs
    else:
        hn_ref, stg_ref, stv_ref, wg_ref, wv_ref, cwg_ref, cwv_ref, cbg_ref, cbv_ref, wd_ref, f_ref, ug_ref, uv_ref = refs
    i = pl.program_id(0)
    j = pl.program_id(1)
    tm = hn_ref.shape[0]

    @pl.when(j == 0)
    def _():
        f_ref[...] = jnp.zeros_like(f_ref)

    if seq_len is None:
        @pl.when(j == 0)
        def _():
            halo = halo_ref[...]
            xe_ref[0:FFN_HALO, :] = jnp.where(i > 0, halo, jnp.zeros_like(halo))
            xe_ref[FFN_HALO:, :] = hn_ref[...]

        def conv(w_ref, cw_ref, cb_ref, u_ref):
            up = _dot(xe_ref[...], w_ref[...])
            cw = cw_ref[...]
            y = (cb_ref[...] + pltpu.roll(up, 2, 0)[FFN_HALO:] * cw[0:1]
                 + pltpu.roll(up, 1, 0)[FFN_HALO:] * cw[1:2] + up[FFN_HALO:] * cw[2:3])
            u_ref[...] = up[FFN_HALO + tm - 8:]
            return y
        gate = conv(wg_ref, cwg_ref, cbg_ref, ug_ref)
        val = conv(wv_ref, cwv_ref, cbv_ref, uv_ref)
    else:
        t = lax.broadcasted_iota(jnp.int32, (tm, 1), 0) % seq_len

        def conv(w_ref, cw_ref, cb_ref, st_ref, u_ref):
            up = _dot(hn_ref[...], w_ref[...])
            st = st_ref[...]
            cw = cw_ref[...]
            prev2 = jnp.where(t < 2, st, pltpu.roll(up, 2, 0))
            prev1 = jnp.where(t < 1, pltpu.roll(st, tm - 1, 0), pltpu.roll(up, 1, 0))
            u_ref[...] = up
            return cb_ref[...] + prev2 * cw[0:1] + prev1 * cw[1:2] + up * cw[2:3]
        gate = conv(wg_ref, cwg_ref, cbg_ref, stg_ref, ug_ref)
        val = conv(wv_ref, cwv_ref, cbv_ref, stv_ref, uv_ref)

    f_ref[...] += _dot((_gelu_tanh(gate) * val).astype(BF16), wd_ref[...])


def _ffn(hn, w_up, cw, cb, w_down, state=None, seq_len=None):
    rows, d = hn.shape
    dff = w_down.shape[0]
    tf = FFN_TF
    nj = dff // tf
    tm = min(FFN_TM, rows)
    gcol = lambda r: pl.BlockSpec((r, tf), lambda i, j: (0, j))
    vcol = lambda r: pl.BlockSpec((r, tf), lambda i, j: (0, j + nj))
    hn_spec = pl.BlockSpec((tm, d), lambda i, j: (i, 0))
    common = [pl.BlockSpec((d, tf), lambda i, j: (0, j)), pl.BlockSpec((d, tf), lambda i, j: (0, j + nj)),
              gcol(FFN_CONV), vcol(FFN_CONV), gcol(1), vcol(1),
              pl.BlockSpec((tf, d), lambda i, j: (j, 0))]
    common_args = (w_up, w_up, cw, cw, cb, cb, w_down)
    f_spec = pl.BlockSpec((tm, d), lambda i, j: (i, 0))
    if seq_len is None:
        per = tm // FFN_HALO
        halo = pl.BlockSpec((FFN_HALO, d), lambda i, j: (jnp.maximum(i * per - 1, 0), 0))
        in_specs = [hn_spec, halo] + common
        args = (hn, hn) + common_args
        u_rows = 8 * (rows // tm)
        u_spec = pl.BlockSpec((8, tf), lambda i, j: (i, j))
        scratch = [pltpu.VMEM((FFN_HALO + tm, d), BF16)]
    else:
        assert rows == tm
        in_specs = [hn_spec, pl.BlockSpec((tm, tf), lambda i, j: (0, j)),
                    pl.BlockSpec((tm, tf), lambda i, j: (0, j + nj))] + common
        args = (hn, state, state) + common_args
        u_rows = tm
        u_spec = pl.BlockSpec((u_rows, tf), lambda i, j: (0, j))
        scratch = []
    return pl.pallas_call(
        functools.partial(_ffn_kernel, seq_len=seq_len),
        grid=(rows // tm, nj),
        in_specs=in_specs,
        out_specs=(f_spec, u_spec, u_spec),
        out_shape=(jax.ShapeDtypeStruct((rows, d), F32),
                   jax.ShapeDtypeStruct((u_rows, dff), F32),
                   jax.ShapeDtypeStruct((u_rows, dff), F32)),
        scratch_shapes=scratch,
        compiler_params=_params("arbitrary", "arbitrary"),
        name="ffn",
    )(*args)


def _ple_kernel(h_ref, f_ref, p_ref, gpost_ref, wg_ref, wp_ref, o_ref):
    h = h_ref[...] + _rms(f_ref[...]) * gpost_ref[...]
    gate = 1.0 / (1.0 + jnp.exp(-_dot(h.astype(BF16), wg_ref[...])))
    o_ref[...] = h + _dot(p_ref[...].astype(BF16), wp_ref[...]) * gate


def _ple(h, f, p, gpost, wg, wp):
    rows, d = h.shape
    tm = min(ROW_TILE, rows)
    row = lambda n: pl.BlockSpec((tm, n), lambda i: (i, 0))
    return pl.pallas_call(
        _ple_kernel,
        grid=(rows // tm,),
        in_specs=[row(d), row(d), row(p.shape[1]),
                  _const_spec(gpost.shape), _const_spec(wg.shape), _const_spec(wp.shape)],
        out_specs=row(d),
        out_shape=jax.ShapeDtypeStruct((rows, d), F32),
        compiler_params=_params("parallel"),
        name="ple",
    )(h, f, p, gpost, wg, wp)


def _after_attention(x, p, lw, batch, o, kf, vf, z, xbc, dtr, ssm0, conv0, ffn0):
    rows, d = x.shape
    l = rows // batch
    sample = ffn0 is not None
    y, ssm_new, conv_new = _ssd(xbc, z, dtr, ssm0, conv0, lw["w_conv"], lw["b_conv"], lw["dt_bias"], lw["a_log"],
                                lw["d_skip"], lw["g_ssd_norm"], batch)
    h1, hn = _out_proj(o, y, x, lw["w_out_a"], lw["w_out_b"], lw["g_post_mix"], lw["g_pre_ffn"])
    dff2 = lw["w_up"].shape[1]
    if sample:
        st = jnp.pad(ffn0, ((0, 0), (0, l - (FFN_CONV - 1)), (0, 0))).reshape(rows, dff2)
        f, ug, uv = _ffn(hn, lw["w_up"], lw["w_ffn_conv"], lw["b_ffn_conv"], lw["w_down"], state=st, seq_len=l)
        up = jnp.concatenate([ug, uv], axis=1).reshape(batch, l, dff2)
        ffn_new = up[:, l - (FFN_CONV - 1):, :]
    else:
        f, ug, uv = _ffn(hn, lw["w_up"], lw["w_ffn_conv"], lw["b_ffn_conv"], lw["w_down"])
        ffn_new = jnp.concatenate([ug, uv], axis=1)[None, -(FFN_CONV - 1):, :]
    out = _ple(h1, f, p, lw["g_post_ffn"], lw["w_ple_gate"], lw["w_ple_proj"])
    return out, (kf, vf, ssm_new, conv_new, ffn_new)


def _prep_weights(i, g_pre_mix, w_in, sb_bias, w_conv, b_conv, dt_bias, a_log, d_skip, g_ssd_norm, w_out,
                  g_post_mix, g_pre_ffn, w_up, w_ffn_conv, b_ffn_conv, w_down, g_post_ffn, w_ple_gate, w_ple_proj):
    row = lambda a: a[i][None, :].astype(F32)
    lane_pad = lambda a: jnp.pad(a[i].astype(F32), (0, LANES - a.shape[1]))[None, :]
    w = w_in[i].astype(BF16)
    o_dt = 3 * ATT_WIDTH + SSD_WIDTH + CONV_DIM
    return {
        "g_pre_mix": row(g_pre_mix),
        "w_in": w,
        "w_dt": jnp.pad(w[:, o_dt:], ((0, 0), (0, LANES - SSD_HEADS))),
        "sb_bias": sb_bias[i].astype(F32),
        "w_conv": w_conv[i].astype(F32), "b_conv": row(b_conv),
        "dt_bias": lane_pad(dt_bias), "a_log": lane_pad(a_log),
        "d_skip": jnp.repeat(d_skip[i].astype(F32), SSD_HEAD_DIM)[None, :],
        "g_ssd_norm": row(g_ssd_norm),
        "w_out_a": w_out[i, :ATT_WIDTH].astype(BF16), "w_out_b": w_out[i, ATT_WIDTH:].astype(BF16),
        "g_post_mix": row(g_post_mix), "g_pre_ffn": row(g_pre_ffn),
        "w_up": w_up[i].astype(BF16), "w_ffn_conv": w_ffn_conv[i].astype(F32), "b_ffn_conv": row(b_ffn_conv),
        "w_down": w_down[i].astype(BF16), "g_post_ffn": row(g_post_ffn),
        "w_ple_gate": w_ple_gate[i].astype(BF16), "w_ple_proj": w_ple_proj[i].astype(BF16),
    }


def kernel(x_prompt, x_sample, cache_k, cache_v, state_ssm, state_conv, state_ffn_conv, page_table, p_prompt, p_sample, g_pre_mix, w_in, sb_bias, w_conv, b_conv, dt_bias, a_log, d_skip, g_ssd_norm, w_out, g_post_mix, g_pre_ffn, w_up, w_ffn_conv, b_ffn_conv, w_down, g_post_ffn, w_ple_gate, w_ple_proj):
    bp, lp, d = x_prompt.shape
    db, ls, _ = x_sample.shape
    depth = w_in.shape[0]
    dff2 = w_up.shape[2]
    hp = x_prompt.reshape(bp * lp, d)
    hs = x_sample.reshape(db * ls, d)
    outs_p, outs_s = [], []
    for i in range(depth):
        lw = _prep_weights(i, g_pre_mix, w_in, sb_bias, w_conv, b_conv, dt_bias, a_log, d_skip, g_ssd_norm, w_out,
                           g_post_mix, g_pre_ffn, w_up, w_ffn_conv, b_ffn_conv, w_down, g_post_ffn,
                           w_ple_gate, w_ple_proj)
        ssm0 = jnp.zeros((bp, SSD_HEADS, SSD_HEAD_DIM, SSD_STATE), F32)
        conv0 = jnp.zeros((bp, SSD_CONV - 1, CONV_DIM), F32)
        proj = lambda x, q_dtype: _in_proj(x, lw["g_pre_mix"], lw["w_in"], lw["w_dt"], q_dtype)
        q_p, kf_p, vf_p, kb_p, vb_p, z_p, xbc_p, dt_p = proj(hp, BF16)
        q_s, kf_s, vf_s, _, _, z_s, xbc_s, dt_s = proj(hs, F32)
        seq = lambda a: a.reshape(db, ls, ATT_WIDTH)
        o_p, o_s = _attention(lw["sb_bias"], page_table, q_p, kb_p, vb_p, seq(q_s), seq(kf_s), seq(vf_s),
                              cache_k[i], cache_v[i])
        hp, st_p = _after_attention(hp, p_prompt[i].reshape(bp * lp, -1), lw, bp, o_p, kf_p, vf_p, z_p, xbc_p, dt_p,
                                    ssm0, conv0, None)
        hs, st_s = _after_attention(hs, p_sample[i].reshape(db * ls, -1), lw, db, o_s.reshape(db * ls, ATT_WIDTH),
                                    kf_s, vf_s, z_s, xbc_s, dt_s, state_ssm[i], state_conv[i], state_ffn_conv[i])
        outs_p.append(st_p)
        outs_s.append(st_s)

    def stack(outs, j, shape):
        return jnp.stack([o[j].reshape(shape) for o in outs], axis=0)

    kv_p = (bp, lp, ATT_HEADS, ATT_HEAD_DIM)
    kv_s = (db, ls, ATT_HEADS, ATT_HEAD_DIM)
    ssm_p = (bp, SSD_HEADS, SSD_HEAD_DIM, SSD_STATE)
    ssm_s = (db, SSD_HEADS, SSD_HEAD_DIM, SSD_STATE)
    return (hp.reshape(bp, lp, d), hs.reshape(db, ls, d),
            stack(outs_p, 0, kv_p), stack(outs_s, 0, kv_s), stack(outs_p, 1, kv_p), stack(outs_s, 1, kv_s),
            stack(outs_p, 2, ssm_p), stack(outs_s, 2, ssm_s),
            stack(outs_p, 3, (bp, SSD_CONV - 1, CONV_DIM)), stack(outs_s, 3, (db, SSD_CONV - 1, CONV_DIM)),
            stack(outs_p, 4, (bp, FFN_CONV - 1, dff2)), stack(outs_s, 4, (db, FFN_CONV - 1, dff2)))
```
